```python
import math
import jax, jax.numpy as jnp
from jax import lax
import numpy as np

D_MODEL = 4096
BATCH = 4
SEQ = 2048
DEPTH = 2
DEC_BATCH = 128
DEC_SEQ = 1
PAST_LEN = 16384
PAGE_SIZE = 128

N_EVEN = (DEPTH + 1) // 2
N_ODD = DEPTH // 2
RET_WIDTH = D_MODEL // 2
RET_HEADS = 8
RET_DK = RET_WIDTH // RET_HEADS
RET_DV = RET_WIDTH // RET_HEADS
RET_CHUNK = 128
S5_WIDTH = D_MODEL // 2
S5_GROUP = 16
S5_GROUPS = S5_WIDTH // S5_GROUP
S5_STATE = 64
AB_PROJ = 4 * RET_WIDTH + S5_WIDTH
MLA_HEADS = 32
QK_NOPE = 128
QK_ROPE = 64
V_HEAD = 128
Q_LORA = 1024
KV_LORA = 512
C_PROJ = Q_LORA + KV_LORA + QK_ROPE
Q_BLOCK = 128
ATTN_SCALE = (QK_NOPE + QK_ROPE) ** -0.5
ROPE_THETA = 10000.0
N_EXPERTS = 32
N_EXPERT_GROUPS = 4
EXPERTS_PER_GROUP = N_EXPERTS // N_EXPERT_GROUPS
GROUP_SCORE_TOPK = 2
TOP_K = 2
EXPERT_FF = 1024
NORM_EPS = 1e-6
GN_EPS = 1e-5
NEG_INF = -1e30
F32 = jnp.float32

kernel_name = 'hybrid_retention_s5_mla_moe_step'


def rmsnorm(x, g):
    xf = x.astype(F32)
    y = xf * lax.rsqrt(jnp.mean(xf * xf, axis=-1, keepdims=True) + NORM_EPS)
    return (y * g.astype(F32)).astype(x.dtype)


def adaln(c, w, b):
    mod = jax.nn.silu(c.astype(F32)) @ w.astype(F32) + b.astype(F32)
    return jnp.split(mod, 6, axis=-1)


def modulate(h, shift, scale):
    return (h.astype(F32) * (1.0 + scale[:, None, :]) + shift[:, None, :]).astype(h.dtype)


def rope(x, pos):
    half = x.shape[-1] // 2
    inv = jnp.exp(-math.log(ROPE_THETA) * jnp.arange(half, dtype=F32) / half)
    ang = pos[:, None] * inv[None, :]
    cos = jnp.cos(ang)[None, :, None, :]
    sin = jnp.sin(ang)[None, :, None, :]
    xf = x.astype(F32)
    x1, x2 = xf[..., :half], xf[..., half:]
    return jnp.concatenate([x1 * cos - x2 * sin, x1 * sin + x2 * cos], axis=-1).astype(x.dtype)


def retention_chunked(q, k, v, s0, chunk):
    b, l, h, _ = q.shape
    dv = v.shape[-1]
    nc = l // chunk
    lg = jnp.log1p(-jnp.exp2(-5.0 - jnp.arange(h, dtype=F32)))
    idx = jnp.arange(chunk, dtype=F32)
    diff = idx[:, None] - idx[None, :]
    intra = jnp.where(diff[None] >= 0, jnp.exp(jnp.maximum(diff, 0.0)[None] * lg[:, None, None]), 0.0)
    q_dec = jnp.exp((idx + 1.0)[None, :] * lg[:, None])
    k_dec = jnp.exp((chunk - 1.0 - idx)[None, :] * lg[:, None])
    c_dec = jnp.exp(chunk * lg)

    def to_chunks(t):
        return t.astype(F32).reshape(b, nc, chunk, h, -1).transpose(1, 0, 3, 2, 4)

    def step(s, qkv):
        qc, kc, vc = qkv
        att = jnp.einsum('bhid,bhjd->bhij', qc, kc) * intra
        o = (jnp.einsum('bhij,bhje->bhie', att, vc)
             + jnp.einsum('bhid,bhde->bhie', qc, s) * q_dec[None, :, :, None])
        s = s * c_dec[None, :, None, None] + jnp.einsum('bhjd,bhje->bhde', kc * k_dec[None, :, :, None], vc)
        return s, o

    s_fin, o = lax.scan(step, s0.astype(F32), (to_chunks(q), to_chunks(k), to_chunks(v)))
    o = o.transpose(1, 0, 3, 2, 4).reshape(b, l, h, dv)
    return o, s_fin


def head_norm(o, g):
    mu = jnp.mean(o, axis=-1, keepdims=True)
    var = jnp.mean(jnp.square(o - mu), axis=-1, keepdims=True)
    o = (o - mu) * lax.rsqrt(var + GN_EPS)
    return o.reshape(o.shape[0], o.shape[1], -1) * g.astype(F32)


def s5_scan(u, h0_re, h0_im, lam_re, lam_im, log_dt, b_re, b_im, c_re, c_im, d):
    uf = u.astype(F32)
    lam = lax.complex(lam_re.astype(F32), lam_im.astype(F32))
    dt = jnp.exp(log_dt.astype(F32))[:, None]
    a_bar = jnp.exp(lam * dt)
    b_bar = ((a_bar - 1.0) / lam)[:, :, None] * lax.complex(b_re.astype(F32), b_im.astype(F32))
    c_mat = lax.complex(c_re.astype(F32), c_im.astype(F32))
    bu = jnp.einsum('gpc,blgc->blgp', b_bar, uf.astype(jnp.complex64))
    h0 = lax.complex(h0_re.astype(F32), h0_im.astype(F32))
    bu = bu.at[:, 0].add(a_bar[None] * h0)
    a = jnp.broadcast_to(a_bar, bu.shape)

    def combine(e1, e2):
        a1, b1 = e1
        a2, b2 = e2
        return a1 * a2, a2 * b1 + b2

    _, hs = lax.associative_scan(combine, (a, bu), axis=1)
    y = jnp.einsum('gcp,blgp->blgc', c_mat, hs).real + d.astype(F32) * uf
    h_last = hs[:, -1]
    return y, h_last.real, h_last.imag


def even_mixer(h, pos, ret_s0, s5_re0, s5_im0, chunk, w_in, ret_gn, lam_re, lam_im, log_dt,
               b_re, b_im, c_re, c_im, d, w_glu, b_glu, w_out):
    bsz, l, _ = h.shape
    q, k, v, g, u = jnp.split(h @ w_in, [RET_WIDTH, 2 * RET_WIDTH, 3 * RET_WIDTH, 4 * RET_WIDTH], axis=-1)
    q = rope(q.reshape(bsz, l, RET_HEADS, RET_DK), pos)
    k = rope(k.reshape(bsz, l, RET_HEADS, RET_DK), pos) * (RET_DK ** -0.5)
    o, ret_s = retention_chunked(q, k, v.reshape(bsz, l, RET_HEADS, RET_DV), ret_s0, chunk)
    ret_out = jax.nn.silu(g.astype(F32)) * head_norm(o, ret_gn)
    y, s5_re, s5_im = s5_scan(u.reshape(bsz, l, S5_GROUPS, S5_GROUP), s5_re0, s5_im0,
                              lam_re, lam_im, log_dt, b_re, b_im, c_re, c_im, d)
    z = jax.nn.gelu(y.reshape(bsz, l, S5_WIDTH))
    s5_out = z * jax.nn.sigmoid(z @ w_glu.astype(F32) + b_glu.astype(F32))
    mixed = jnp.concatenate([ret_out, s5_out], axis=-1).astype(h.dtype) @ w_out
    return mixed, ret_s, s5_re, s5_im


def mla_project(h, pos, w_in, q_norm, kv_norm, w_q_up):
    proj = h @ w_in
    q_lat = rmsnorm(proj[..., :Q_LORA], q_norm)
    ckv = rmsnorm(proj[..., Q_LORA:Q_LORA + KV_LORA], kv_norm)
    kpe = rope(proj[..., Q_LORA + KV_LORA:][:, :, None, :], pos)[:, :, 0]
    q = jnp.einsum('blr,rhe->blhe', q_lat, w_q_up)
    q_nope = q[..., :QK_NOPE]
    q_pe = rope(q[..., QK_NOPE:], pos)
    return q_nope, q_pe, ckv, kpe


def mla_prompt(q_nope, q_pe, ckv, kpe, w_kv_up):
    b, l, h, _ = q_nope.shape
    kv = jnp.einsum('blr,rhe->blhe', ckv, w_kv_up)
    k_nope, v = kv[..., :QK_NOPE], kv[..., QK_NOPE:]
    nb = l // Q_BLOCK
    kpos = jnp.arange(l)

    def block(args):
        qn, qp, i = args
        s = jnp.einsum('bqhd,bkhd->bhqk', qn, k_nope) + jnp.einsum('bqhd,bkd->bhqk', qp, kpe)
        s = s.astype(F32) * ATTN_SCALE
        qpos = i * Q_BLOCK + jnp.arange(Q_BLOCK)
        s = jnp.where(kpos[None, :] <= qpos[:, None], s, NEG_INF)
        p = jax.nn.softmax(s, axis=-1).astype(v.dtype)
        return jnp.einsum('bhqk,bkhe->bqhe', p, v)

    qn_b = q_nope.reshape(b, nb, Q_BLOCK, h, QK_NOPE).swapaxes(0, 1)
    qp_b = q_pe.reshape(b, nb, Q_BLOCK, h, QK_ROPE).swapaxes(0, 1)
    o = lax.map(block, (qn_b, qp_b, jnp.arange(nb)))
    return o.swapaxes(0, 1).reshape(b, l, h * V_HEAD)


def mla_sample(q_nope, q_pe, ckv_new, kpe_new, ckv_pool, kpe_pool, page_table, w_kv_up):
    db, t, h, _ = q_nope.shape
    w_uk, w_uv = w_kv_up[..., :QK_NOPE], w_kv_up[..., QK_NOPE:]
    q_abs = jnp.einsum('bqhd,rhd->bqhr', q_nope, w_uk)
    ckv_past = ckv_pool[page_table].reshape(db, -1, KV_LORA)
    kpe_past = kpe_pool[page_table].reshape(db, -1, QK_ROPE)
    past_len = ckv_past.shape[1]
    s_past = jnp.einsum('bqhr,bkr->bhqk', q_abs, ckv_past) + jnp.einsum('bqhd,bkd->bhqk', q_pe, kpe_past)
    s_new = jnp.einsum('bqhr,bkr->bhqk', q_abs, ckv_new) + jnp.einsum('bqhd,bkd->bhqk', q_pe, kpe_new)
    causal = jnp.arange(t)[None, :] <= jnp.arange(t)[:, None]
    s_new = jnp.where(causal, s_new.astype(F32) * ATTN_SCALE, NEG_INF)
    s = jnp.concatenate([s_past.astype(F32) * ATTN_SCALE, s_new], axis=-1)
    p = jax.nn.softmax(s, axis=-1).astype(ckv_past.dtype)
    o_lat = (jnp.einsum('bhqk,bkr->bqhr', p[..., :past_len], ckv_past)
             + jnp.einsum('bhqk,bkr->bqhr', p[..., past_len:].astype(ckv_new.dtype), ckv_new))
    o = jnp.einsum('bqhr,rhe->bqhe', o_lat, w_uv)
    return o.reshape(db, t, h * V_HEAD)


def moe(h, w_router, b_router, w_gate, w_up, w_down):
    b, l, d = h.shape
    hf = h.reshape(b * l, d)
    scores = jax.nn.sigmoid((hf @ w_router).astype(F32))
    sel = (scores + b_router.astype(F32)).reshape(-1, N_EXPERT_GROUPS, EXPERTS_PER_GROUP)
    group_score = lax.top_k(sel, GROUP_SCORE_TOPK)[0].sum(-1)
    g_idx = jnp.argmax(group_score, axis=-1)
    in_group = jnp.take_along_axis(sel, g_idx[:, None, None], axis=1)[:, 0]
    _, e_local = lax.top_k(in_group, TOP_K)
    e_idx = g_idx[:, None] * EXPERTS_PER_GROUP + e_local
    w_sel = jnp.take_along_axis(scores, e_idx, axis=-1)
    w_sel = w_sel / jnp.sum(w_sel, axis=-1, keepdims=True)
    combine = jnp.einsum('tk,tke->te', w_sel, jax.nn.one_hot(e_idx, N_EXPERTS, dtype=F32))
    gate = jnp.einsum('td,edf->tef', hf, w_gate)
    up = jnp.einsum('td,edf->tef', hf, w_up)
    act = jax.nn.silu(gate) * up * combine[:, :, None].astype(gate.dtype)
    out = jnp.einsum('tef,efd->td', act, w_down)
    return out.reshape(b, l, d)


def setup_inputs(seed: int = 0) -> dict:
    key = jax.random.key(seed)
    keys = jax.random.split(key, 40)

    def nrm(i, shape, scale=1.0):
        return jax.random.normal(keys[i], shape, F32) * scale

    n_pages = PAST_LEN // PAGE_SIZE
    n_pool = (DEC_BATCH * n_pages * 5) // 4
    page_table = jax.random.permutation(keys[9], n_pool)[: DEC_BATCH * n_pages].reshape(DEC_BATCH, n_pages).astype(jnp.int32)
    lam_im0 = jnp.pi * jnp.arange(S5_STATE, dtype=F32)
    s5_shape = (N_EVEN, S5_GROUPS, S5_STATE)
    return {
        'x_prompt': nrm(0, (BATCH, SEQ, D_MODEL)),
        'x_sample': nrm(1, (DEC_BATCH, DEC_SEQ, D_MODEL)),
        'c_prompt': nrm(2, (BATCH, D_MODEL)),
        'c_sample': nrm(3, (DEC_BATCH, D_MODEL)),
        'state_ret': nrm(4, (N_EVEN, DEC_BATCH, RET_HEADS, RET_DK, RET_DV), RET_DK ** -0.5),
        'state_s5_re': nrm(5, (N_EVEN, DEC_BATCH, S5_GROUPS, S5_STATE), 0.5),
        'state_s5_im': nrm(6, (N_EVEN, DEC_BATCH, S5_GROUPS, S5_STATE), 0.5),
        'cache_ckv': nrm(7, (N_ODD, n_pool, PAGE_SIZE, KV_LORA)),
        'cache_kpe': nrm(8, (N_ODD, n_pool, PAGE_SIZE, QK_ROPE)),
        'page_table': page_table,
        'w_ada': nrm(10, (DEPTH, D_MODEL, 6 * D_MODEL), 0.5 * D_MODEL ** -0.5),
        'b_ada': nrm(11, (DEPTH, 6 * D_MODEL), 0.02),
        'norm_mix': 1.0 + nrm(12, (DEPTH, D_MODEL), 0.02),
        'norm_ffn': 1.0 + nrm(13, (DEPTH, D_MODEL), 0.02),
        'norm_final': 1.0 + nrm(14, (D_MODEL,), 0.02),
        'w_in_ab': nrm(15, (N_EVEN, D_MODEL, AB_PROJ), D_MODEL ** -0.5),
        'ret_gn': 1.0 + nrm(16, (N_EVEN, RET_WIDTH), 0.02),
        's5_lam_re': -0.5 + nrm(17, s5_shape, 0.01),
        's5_lam_im': lam_im0 + nrm(18, s5_shape, 0.01),
        's5_log_dt': jax.random.uniform(keys[19], (N_EVEN, S5_GROUPS), F32, math.log(1e-3), math.log(1e-1)),
        's5_b_re': nrm(20, (N_EVEN, S5_GROUPS, S5_STATE, S5_GROUP), (2 * S5_GROUP) ** -0.5),
        's5_b_im': nrm(21, (N_EVEN, S5_GROUPS, S5_STATE, S5_GROUP), (2 * S5_GROUP) ** -0.5),
        's5_c_re': nrm(22, (N_EVEN, S5_GROUPS, S5_GROUP, S5_STATE), S5_STATE ** -0.5),
        's5_c_im': nrm(23, (N_EVEN, S5_GROUPS, S5_GROUP, S5_STATE), S5_STATE ** -0.5),
        's5_d': nrm(24, (N_EVEN, S5_GROUPS, S5_GROUP)),
        's5_w_glu': nrm(25, (N_EVEN, S5_WIDTH, S5_WIDTH), S5_WIDTH ** -0.5),
        's5_b_glu': nrm(26, (N_EVEN, S5_WIDTH), 0.02),
        'w_out_ab': nrm(27, (N_EVEN, RET_WIDTH + S5_WIDTH, D_MODEL), (RET_WIDTH + S5_WIDTH) ** -0.5),
        'w_in_c': nrm(28, (N_ODD, D_MODEL, C_PROJ), D_MODEL ** -0.5),
        'q_norm': 1.0 + nrm(29, (N_ODD, Q_LORA), 0.02),
        'kv_norm': 1.0 + nrm(30, (N_ODD, KV_LORA), 0.02),
        'w_q_up': nrm(31, (N_ODD, Q_LORA, MLA_HEADS, QK_NOPE + QK_ROPE), Q_LORA ** -0.5),
        'w_kv_up': nrm(32, (N_ODD, KV_LORA, MLA_HEADS, QK_NOPE + V_HEAD), KV_LORA ** -0.5),
        'w_out_c': nrm(33, (N_ODD, MLA_HEADS * V_HEAD, D_MODEL), (MLA_HEADS * V_HEAD) ** -0.5),
        'w_router': nrm(34, (D_MODEL, N_EXPERTS), D_MODEL ** -0.5),
        'b_router': nrm(35, (N_EXPERTS,), 0.01),
        'w_exp_gate': nrm(36, (DEPTH, N_EXPERTS, D_MODEL, EXPERT_FF), D_MODEL ** -0.5),
        'w_exp_up': nrm(37, (DEPTH, N_EXPERTS, D_MODEL, EXPERT_FF), D_MODEL ** -0.5),
        'w_exp_down': nrm(38, (DEPTH, N_EXPERTS, EXPERT_FF, D_MODEL), EXPERT_FF ** -0.5),
    }


def reference(x_prompt, x_sample, c_prompt, c_sample, state_ret, state_s5_re, state_s5_im,
              cache_ckv, cache_kpe, page_table, w_ada, b_ada, norm_mix, norm_ffn, norm_final,
              w_in_ab, ret_gn, s5_lam_re, s5_lam_im, s5_log_dt, s5_b_re, s5_b_im, s5_c_re, s5_c_im,
              s5_d, s5_w_glu, s5_b_glu, w_out_ab, w_in_c, q_norm, kv_norm, w_q_up, w_kv_up, w_out_c,
              w_router, b_router, w_exp_gate, w_exp_up, w_exp_down):
    xp, xs = x_prompt, x_sample
    bp, lp = xp.shape[0], xp.shape[1]
    bs, ls = xs.shape[0], xs.shape[1]
    past_len = page_table.shape[1] * PAGE_SIZE
    pos_p = jnp.arange(lp, dtype=F32)
    pos_s = past_len + jnp.arange(ls, dtype=F32)
    ret_p, ret_s, s5re_p, s5im_p, s5re_s, s5im_s = [], [], [], [], [], []
    ckv_p, kpe_p, ckv_s, kpe_s = [], [], [], []
    for i in range(DEPTH):
        sh1_p, sc1_p, g1_p, sh2_p, sc2_p, g2_p = adaln(c_prompt, w_ada[i], b_ada[i])
        sh1_s, sc1_s, g1_s, sh2_s, sc2_s, g2_s = adaln(c_sample, w_ada[i], b_ada[i])
        h_p = modulate(rmsnorm(xp, norm_mix[i]), sh1_p, sc1_p)
        h_s = modulate(rmsnorm(xs, norm_mix[i]), sh1_s, sc1_s)
        if i % 2 == 0:
            j = i // 2
            ev = (w_in_ab[j], ret_gn[j], s5_lam_re[j], s5_lam_im[j], s5_log_dt[j], s5_b_re[j], s5_b_im[j],
                  s5_c_re[j], s5_c_im[j], s5_d[j], s5_w_glu[j], s5_b_glu[j], w_out_ab[j])
            zero_ret = jnp.zeros((bp, RET_HEADS, RET_DK, RET_DV), F32)
            zero_s5 = jnp.zeros((bp, S5_GROUPS, S5_STATE), F32)
            m_p, r_p, sr_p, si_p = even_mixer(h_p, pos_p, zero_ret, zero_s5, zero_s5, RET_CHUNK, *ev)
            m_s, r_s, sr_s, si_s = even_mixer(h_s, pos_s, state_ret[j], state_s5_re[j], state_s5_im[j], ls, *ev)
            ret_p.append(r_p)
            ret_s.append(r_s)
            s5re_p.append(sr_p)
            s5im_p.append(si_p)
            s5re_s.append(sr_s)
            s5im_s.append(si_s)
        else:
            j = i // 2
            qn_p, qp_p, cv_p, kp_p = mla_project(h_p, pos_p, w_in_c[j], q_norm[j], kv_norm[j], w_q_up[j])
            m_p = mla_prompt(qn_p, qp_p, cv_p, kp_p, w_kv_up[j]) @ w_out_c[j]
            qn_s, qp_s, cv_s, kp_s = mla_project(h_s, pos_s, w_in_c[j], q_norm[j], kv_norm[j], w_q_up[j])
            m_s = mla_sample(qn_s, qp_s, cv_s, kp_s, cache_ckv[j], cache_kpe[j], page_table, w_kv_up[j]) @ w_out_c[j]
            ckv_p.append(cv_p)
            kpe_p.append(kp_p)
            ckv_s.append(cv_s)
            kpe_s.append(kp_s)
        xp = (xp + g1_p[:, None, :] * m_p).astype(x_prompt.dtype)
        xs = (xs + g1_s[:, None, :] * m_s).astype(x_sample.dtype)
        f_p = modulate(rmsnorm(xp, norm_ffn[i]), sh2_p, sc2_p)
        f_s = modulate(rmsnorm(xs, norm_ffn[i]), sh2_s, sc2_s)
        xp = (xp + g2_p[:, None, :] * moe(f_p, w_router, b_router, w_exp_gate[i], w_exp_up[i], w_exp_down[i])).astype(x_prompt.dtype)
        xs = (xs + g2_s[:, None, :] * moe(f_s, w_router, b_router, w_exp_gate[i], w_exp_up[i], w_exp_down[i])).astype(x_sample.dtype)
    y_prompt = rmsnorm(xp, norm_final)
    y_sample = rmsnorm(xs, norm_final)
    return (y_prompt, y_sample, jnp.stack(ret_p), jnp.stack(ret_s), jnp.stack(s5re_p), jnp.stack(s5im_p),
            jnp.stack(s5re_s), jnp.stack(s5im_s), jnp.stack(ckv_p), jnp.stack(kpe_p), jnp.stack(ckv_s), jnp.stack(kpe_s))
```

```python
import functools
import math

import jax
import jax.numpy as jnp
from jax import lax
from jax.experimental import pallas as pl
from jax.experimental.pallas import tpu as pltpu

F32 = jnp.float32
BF16 = jnp.bfloat16
I32 = jnp.int32

NORM_EPS = 1e-6
GN_EPS = 1e-5
NEG_INF = -1e30
ROPE_THETA = 10000.0
RET_CHUNK = 128
N_EXPERT_GROUPS = 4
LANES = 128
SUBLANES = 8
VMEM_LIMIT = 56 * 1024 * 1024
MOE_TM = 256
RET_PAD = 16


def _cp(sem):
    return pltpu.CompilerParams(dimension_semantics=sem, vmem_limit_bytes=VMEM_LIMIT)


def _bdot(a, b):
    return jnp.dot(a.astype(BF16), b.astype(BF16), preferred_element_type=F32)


def _bdot_nt(a, b):
    return lax.dot_general(a.astype(BF16), b.astype(BF16), (((1,), (1,)), ((), ())),
                           preferred_element_type=F32)


def _bdot_tn(a, b):
    return lax.dot_general(a.astype(BF16), b.astype(BF16), (((0,), (0,)), ((), ())),
                           preferred_element_type=F32)


def _adaln_kernel(c_ref, w_ref, b_ref, o_ref):
    o_ref[...] = _bdot(jax.nn.silu(c_ref[...]), w_ref[...]) + b_ref[...]


def _adaln(c_all, w_ada, b_ada, tn=512):
    depth, d, n = w_ada.shape
    bc = c_all.shape[0]
    return pl.pallas_call(
        _adaln_kernel,
        grid=(depth, n // tn),
        in_specs=[pl.BlockSpec((bc, d), lambda l, j: (0, 0)),
                  pl.BlockSpec((None, d, tn), lambda l, j: (l, 0, j)),
                  pl.BlockSpec((None, 1, tn), lambda l, j: (l, 0, j))],
        out_specs=pl.BlockSpec((None, bc, tn), lambda l, j: (l, 0, j)),
        out_shape=jax.ShapeDtypeStruct((depth, bc, n), F32),
        compiler_params=_cp(("parallel", "parallel")),
        name="adaln",
    )(c_all, w_ada, b_ada.reshape(depth, 1, n))


def _rms(x, g):
    return x * lax.rsqrt(jnp.mean(x * x, axis=-1, keepdims=True) + NORM_EPS) * g


def _mod_spec(m, tl):
    if m.shape[1] == 1:
        return pl.BlockSpec((None, 1, m.shape[2]), lambda b, i: (b, 0, 0))
    return pl.BlockSpec((None, tl, m.shape[2]), lambda b, i: (b, i, 0))


def _norm_mod_kernel(x_ref, g_ref, sh_ref, sc_ref, o_ref):
    y = _rms(x_ref[...], g_ref[...])
    o_ref[...] = (y * (1.0 + sc_ref[...]) + sh_ref[...]).astype(o_ref.dtype)


def _norm_mod(x, g, shift, scale, tl):
    bx, lx, d = x.shape
    return pl.pallas_call(
        _norm_mod_kernel,
        grid=(bx, lx // tl),
        in_specs=[pl.BlockSpec((None, tl, d), lambda b, i: (b, i, 0)),
                  pl.BlockSpec((1, d), lambda b, i: (0, 0)),
                  _mod_spec(shift, tl), _mod_spec(scale, tl)],
        out_specs=pl.BlockSpec((None, tl, d), lambda b, i: (b, i, 0)),
        out_shape=jax.ShapeDtypeStruct((bx, lx, d), BF16),
        compiler_params=_cp(("parallel", "parallel")),
        name="norm_mod",
    )(x, g.reshape(1, d), shift, scale)


def _rmsnorm_kernel(x_ref, g_ref, o_ref):
    o_ref[...] = _rms(x_ref[...], g_ref[...])


def _rmsnorm(x, g, tl):
    bx, lx, d = x.shape
    return pl.pallas_call(
        _rmsnorm_kernel,
        grid=(bx, lx // tl),
        in_specs=[pl.BlockSpec((None, tl, d), lambda b, i: (b, i, 0)),
                  pl.BlockSpec((1, d), lambda b, i: (0, 0))],
        out_specs=pl.BlockSpec((None, tl, d), lambda b, i: (b, i, 0)),
        out_shape=jax.ShapeDtypeStruct((bx, lx, d), F32),
        compiler_params=_cp(("parallel", "parallel")),
        name="rmsnorm",
    )(x, g.reshape(1, d))


def _first_argmax(v, iota, size):
    m = jnp.max(v, axis=1, keepdims=True)
    pos = jnp.min(jnp.where(v == m, iota, size), axis=1, keepdims=True)
    return m, pos


def _ffn_norm_route_kernel(x_ref, g_ref, sh_ref, sc_ref, wr_ref, br_ref, f_ref, e_ref, w_ref, *, n_groups):
    y = _rms(x_ref[...], g_ref[...])
    f = y * (1.0 + sc_ref[...]) + sh_ref[...]
    f_ref[...] = f
    logits = lax.dot_general(wr_ref[...], f, (((1,), (1,)), ((), ())),
                             precision=lax.Precision.HIGHEST,
                             preferred_element_type=F32)
    n_exp, tl = logits.shape
    epg = n_exp // n_groups
    scores = jax.nn.sigmoid(logits)
    sel = (scores + br_ref[...]).reshape(n_groups, epg, tl)
    sc3 = scores.reshape(n_groups, epg, tl)
    iota_e = lax.broadcasted_iota(I32, (n_groups, epg, tl), 1)
    m1, p1 = _first_argmax(sel, iota_e, epg)
    m2, p2 = _first_argmax(jnp.where(iota_e == p1, -jnp.inf, sel), iota_e, epg)
    gscore = m1 + m2
    iota_g = lax.broadcasted_iota(I32, (n_groups, 1, tl), 0)
    gmax = jnp.max(gscore, axis=0, keepdims=True)
    g_idx = jnp.min(jnp.where(gscore == gmax, iota_g, n_groups), axis=0, keepdims=True)
    sc_g, p1_g, p2_g = sc3[0:1], p1[0:1], p2[0:1]
    for gi in range(1, n_groups):
        pick = g_idx == gi
        sc_g = jnp.where(pick, sc3[gi:gi + 1], sc_g)
        p1_g = jnp.where(pick, p1[gi:gi + 1], p1_g)
        p2_g = jnp.where(pick, p2[gi:gi + 1], p2_g)
    iota_l = lax.broadcasted_iota(I32, (1, epg, tl), 1)
    s1 = jnp.sum(jnp.where(iota_l == p1_g, sc_g, 0.0), axis=1, keepdims=True)
    s2 = jnp.sum(jnp.where(iota_l == p2_g, sc_g, 0.0), axis=1, keepdims=True)
    tot = s1 + s2
    e_ref[0:1, :] = (g_idx * epg + p1_g).reshape(1, tl)
    e_ref[1:2, :] = (g_idx * epg + p2_g).reshape(1, tl)
    w_ref[0:1, :] = (s1 / tot).reshape(1, tl)
    w_ref[1:2, :] = (s2 / tot).reshape(1, tl)


def _ffn_norm_route(x, g, shift, scale, w_router_t, b_router, tl):
    bx, lx, d = x.shape
    n_exp = w_router_t.shape[0]
    nl = lx // tl
    return pl.pallas_call(
        functools.partial(_ffn_norm_route_kernel, n_groups=N_EXPERT_GROUPS),
        grid=(bx, nl),
        in_specs=[pl.BlockSpec((None, tl, d), lambda b, i: (b, i, 0)),
                  pl.BlockSpec((1, d), lambda b, i: (0, 0)),
                  _mod_spec(shift, tl), _mod_spec(scale, tl),
                  pl.BlockSpec((n_exp, d), lambda b, i: (0, 0)),
                  pl.BlockSpec((n_exp, 1), lambda b, i: (0, 0))],
        out_specs=[pl.BlockSpec((None, tl, d), lambda b, i: (b, i, 0)),
                   pl.BlockSpec((2, tl), lambda b, i: (0, b * nl + i)),
                   pl.BlockSpec((2, tl), lambda b, i: (0, b * nl + i))],
        out_shape=[jax.ShapeDtypeStruct((bx, lx, d), F32),
                   jax.ShapeDtypeStruct((2, bx * lx), I32),
                   jax.ShapeDtypeStruct((2, bx * lx), F32)],
        compiler_params=_cp(("parallel", "parallel")),
        name="ffn_norm_route",
    )(x, g.reshape(1, d), shift, scale, w_router_t, b_router.reshape(n_exp, 1))


def _mm_kernel(*refs, n_a, epi):
    a_refs = refs[:n_a]
    w_ref = refs[n_a]
    extra = refs[n_a + 1:-1]
    o_ref = refs[-1]
    acc = None
    k0 = 0
    for a_ref in a_refs:
        kk = a_ref.shape[-1]
        part = _bdot(a_ref[...], w_ref[k0:k0 + kk, :])
        acc = part if acc is None else acc + part
        k0 += kk
    if epi == "none":
        out = acc
    elif epi == "resgate":
        res_ref, gate_ref = extra
        out = res_ref[...] + gate_ref[...] * acc
    elif epi == "glu":
        z_ref, b_ref = extra
        out = z_ref[...] * jax.nn.sigmoid(acc + b_ref[...])
    elif epi == "rope":
        cos_ref, sin_ref = extra
        parts = []
        for h in range(acc.shape[1] // (2 * LANES)):
            parts.append(acc[:, 2 * h * LANES:(2 * h + 1) * LANES])
            t = acc[:, (2 * h + 1) * LANES:(2 * h + 2) * LANES]
            parts.append(t * cos_ref[...] + pltpu.roll(t, LANES // 2, 1) * sin_ref[...])
        out = jnp.concatenate(parts, axis=1)
    elif epi == "kcat":
        kpe_ref, = extra
        out = jnp.concatenate([acc, kpe_ref[...]], axis=1)
    o_ref[...] = out.astype(o_ref.dtype)


def _mm(a_list, w, *, tm, tn, out_dtype, epi="none", extra=(), extra_specs=(), tn_out=None, name="mm"):
    bx, lx, _ = a_list[0].shape
    tn_out = tn if tn_out is None else tn_out
    in_specs = [pl.BlockSpec((None, tm, a.shape[2]), lambda b, i, j: (b, i, 0)) for a in a_list]
    if isinstance(w, tuple):
        w, layer = w
        _, k, n = w.shape
        in_specs.append(pl.BlockSpec((None, k, tn), lambda b, i, j: (layer, 0, j)))
    else:
        k, n = w.shape
        in_specs.append(pl.BlockSpec((k, tn), lambda b, i, j: (0, j)))
    n_out = n // tn * tn_out
    in_specs.extend(extra_specs)
    return pl.pallas_call(
        functools.partial(_mm_kernel, n_a=len(a_list), epi=epi),
        grid=(bx, lx // tm, n // tn),
        in_specs=in_specs,
        out_specs=pl.BlockSpec((None, tm, tn_out), lambda b, i, j: (b, i, j)),
        out_shape=jax.ShapeDtypeStruct((bx, lx, n_out), out_dtype),
        compiler_params=_cp(("parallel", "parallel", "arbitrary")),
        name=name,
    )(*a_list, w, *extra)


def _gate_spec(gate, tm, tn):
    if gate.shape[1] == 1:
        return pl.BlockSpec((None, 1, tn), lambda b, i, j: (b, 0, j))
    return pl.BlockSpec((None, tm, tn), lambda b, i, j: (b, i, j))


def _mm_resgate(a_list, w, res, gate, tm, tn, name):
    specs = (pl.BlockSpec((None, tm, tn), lambda b, i, j: (b, i, j)), _gate_spec(gate, tm, tn))
    return _mm(a_list, w, tm=tm, tn=tn, out_dtype=F32, epi="resgate", extra=(res, gate),
               extra_specs=specs, name=name)


def _bmm_kernel(a_ref, w_ref, o_ref):
    o_ref[...] = _bdot(a_ref[...], w_ref[...]).astype(o_ref.dtype)


def _bmm_heads(a, w, a_block_of_head, out_dtype, name):
    m = a.shape[0]
    h, ka, n = w.shape
    return pl.pallas_call(
        _bmm_kernel,
        grid=(h,),
        in_specs=[pl.BlockSpec((m, ka), lambda i: (0, a_block_of_head(i))),
                  pl.BlockSpec((None, ka, n), lambda i: (i, 0, 0))],
        out_specs=pl.BlockSpec((m, n), lambda i: (0, i)),
        out_shape=jax.ShapeDtypeStruct((m, h * n), out_dtype),
        compiler_params=_cp(("parallel",)),
        name=name,
    )(a, w)


def _rope_halves(x, cos, sin):
    half = x.shape[-1] // 2
    x1, x2 = x[:, :half], x[:, half:]
    return jnp.concatenate([x1 * cos - x2 * sin, x1 * sin + x2 * cos], axis=-1)


def _retention_kernel(*refs, has_s0, k_scale):
    if has_s0:
        (q_ref, k_ref, v_ref, g_ref, cos_ref, sin_ref, intra_ref, qdec_ref, kdec_ref, cdec_ref, gn_ref,
         s0_ref, o_ref, sfin_ref, s_scr) = refs
    else:
        (q_ref, k_ref, v_ref, g_ref, cos_ref, sin_ref, intra_ref, qdec_ref, kdec_ref, cdec_ref, gn_ref,
         o_ref, sfin_ref, s_scr) = refs
    c = pl.program_id(2)

    @pl.when(c == 0)
    def _():
        if has_s0:
            s_scr[...] = s0_ref[...]
        else:
            s_scr[...] = jnp.zeros_like(s_scr)

    cos, sin = cos_ref[...], sin_ref[...]
    q = _rope_halves(q_ref[...], cos, sin)
    k = _rope_halves(k_ref[...], cos, sin) * k_scale
    v = v_ref[...]
    s = s_scr[...]
    att = _bdot_nt(q, k) * intra_ref[...]
    o = _bdot(att, v) + _bdot(q, s) * qdec_ref[...]
    s_new = s * cdec_ref[...] + _bdot_tn(k * kdec_ref[...], v)
    s_scr[...] = s_new

    @pl.when(c == pl.num_programs(2) - 1)
    def _():
        sfin_ref[...] = s_new

    mu = jnp.mean(o, axis=-1, keepdims=True)
    var = jnp.mean(jnp.square(o - mu), axis=-1, keepdims=True)
    on = (o - mu) * lax.rsqrt(var + GN_EPS) * gn_ref[...]
    o_ref[...] = (jax.nn.silu(g_ref[...]) * on).astype(o_ref.dtype)


def _ret_tables(n_heads, chunk, pad_to):
    lg = jnp.log1p(-jnp.exp2(-5.0 - jnp.arange(n_heads, dtype=F32)))
    idx = jnp.arange(chunk, dtype=F32)
    diff = idx[:, None] - idx[None, :]
    intra = jnp.where(diff[None] >= 0, jnp.exp(jnp.maximum(diff, 0.0)[None] * lg[:, None, None]), 0.0)
    q_dec = jnp.exp((idx + 1.0)[None, :] * lg[:, None])[:, :, None]
    k_dec = jnp.exp((chunk - 1.0 - idx)[None, :] * lg[:, None])[:, :, None]
    c_dec = jnp.exp(chunk * lg)[:, None, None]
    p = pad_to - chunk
    if p:
        intra = jnp.pad(intra, ((0, 0), (0, p), (0, p)))
        q_dec = jnp.pad(q_dec, ((0, 0), (0, p), (0, 0)))
        k_dec = jnp.pad(k_dec, ((0, 0), (0, p), (0, 0)))
    return intra, q_dec, k_dec, c_dec


def _rope_tables(pos, half):
    inv = jnp.exp(-math.log(ROPE_THETA) * jnp.arange(half, dtype=F32) / half)
    ang = pos[:, None] * inv[None, :]
    return jnp.cos(ang), jnp.sin(ang)


def _retention(proj, cos, sin, tables, gn, s0, n_heads, dk, chunk):
    b, l, _ = proj.shape
    nc = l // chunk
    intra, q_dec, k_dec, c_dec = tables
    has_s0 = s0 is not None

    def col(off):
        return pl.BlockSpec((None, chunk, dk), lambda bi, h, c: (bi, c, off + h))

    in_specs = [col(0), col(n_heads), col(2 * n_heads), col(3 * n_heads),
                pl.BlockSpec((chunk, dk // 2), lambda bi, h, c: (c, 0)),
                pl.BlockSpec((chunk, dk // 2), lambda bi, h, c: (c, 0)),
                pl.BlockSpec((None, chunk, chunk), lambda bi, h, c: (h, 0, 0)),
                pl.BlockSpec((None, chunk, 1), lambda bi, h, c: (h, 0, 0)),
                pl.BlockSpec((None, chunk, 1), lambda bi, h, c: (h, 0, 0)),
                pl.BlockSpec((None, 1, 1), lambda bi, h, c: (h, 0, 0)),
                pl.BlockSpec((1, dk), lambda bi, h, c: (0, h))]
    args = [proj, proj, proj, proj, cos, sin, intra, q_dec, k_dec, c_dec, gn.reshape(1, -1)]
    if has_s0:
        in_specs.append(pl.BlockSpec((None, None, dk, dk), lambda bi, h, c: (bi, h, 0, 0)))
        args.append(s0)
    return pl.pallas_call(
        functools.partial(_retention_kernel, has_s0=has_s0, k_scale=dk ** -0.5),
        grid=(b, n_heads, nc),
        in_specs=in_specs,
        out_specs=[pl.BlockSpec((None, chunk, dk), lambda bi, h, c: (bi, c, h)),
                   pl.BlockSpec((None, None, dk, dk), lambda bi, h, c: (bi, h, 0, 0))],
        out_shape=[jax.ShapeDtypeStruct((b, l, n_heads * dk), BF16),
                   jax.ShapeDtypeStruct((b, n_heads, dk, dk), F32)],
        scratch_shapes=[pltpu.VMEM((dk, dk), F32)],
        compiler_params=_cp(("parallel", "parallel", "arbitrary")),
        name="retention",
    )(*args)


def _cmul(ar, ai, br, bi):
    return ar * br - ai * bi, ar * bi + ai * br


def _s5_params(lam_re, lam_im, log_dt, b_re, b_im, c_re, c_im):
    g, p, gs = b_re.shape
    gpb = LANES // gs
    nj = g // gpb
    lam = lax.complex(lam_re.astype(F32), lam_im.astype(F32))
    dt = jnp.exp(log_dt.astype(F32))[:, None]
    a_bar = jnp.exp(lam * dt)
    b_bar = ((a_bar - 1.0) / lam)[:, :, None] * lax.complex(b_re.astype(F32), b_im.astype(F32))
    eye = jnp.eye(gpb, dtype=F32)

    def bdiag_in(m):
        m = m.reshape(nj, gpb, p, gs).transpose(0, 1, 3, 2)
        return (m[:, :, :, None, :] * eye[None, :, None, :, None]).reshape(nj, gpb * gs, gpb * p)

    def bdiag_out(m):
        m = m.reshape(nj, gpb, gs, p).transpose(0, 1, 3, 2)
        return (m[:, :, :, None, :] * eye[None, :, None, :, None]).reshape(nj, gpb * p, gpb * gs)

    b_blk = jnp.concatenate([bdiag_in(jnp.real(b_bar)), bdiag_in(jnp.imag(b_bar))], axis=2)
    c_blk = jnp.concatenate([bdiag_out(c_re.astype(F32)), bdiag_out(-c_im.astype(F32))], axis=1)
    a_re = jnp.real(a_bar).reshape(nj, 1, gpb * p)
    a_im = jnp.imag(a_bar).reshape(nj, 1, gpb * p)
    return b_blk, c_blk, a_re, a_im


def _s5_scan_kernel(*refs, has_h0, n_seg):
    if has_h0:
        (u_ref, b_ref, c_ref, ar_ref, ai_ref, d_ref, h0r_ref, h0i_ref, z_ref, hr_ref, hi_ref, bu_scr) = refs
    else:
        (u_ref, b_ref, c_ref, ar_ref, ai_ref, d_ref, z_ref, hr_ref, hi_ref, bu_scr) = refs
    l = u_ref.shape[0]
    ns = ar_ref.shape[1]
    seg = l // n_seg
    nct = ns // LANES
    u = u_ref[...]
    bu = _bdot(u, b_ref[...])
    for c in range(2 * nct):
        bu_scr[c] = bu[:, c * LANES:(c + 1) * LANES]
    ar = jnp.broadcast_to(ar_ref[...], (n_seg, ns))
    ai = jnp.broadcast_to(ai_ref[...], (n_seg, ns))

    def rows(t, c0):
        return jnp.concatenate([bu_scr[c0 + c, pl.ds(t, n_seg, stride=seg), :] for c in range(nct)], axis=1)

    def local_step(t, carry):
        hr, hi = carry
        pr, pi = _cmul(ar, ai, hr, hi)
        return pr + rows(t, 0), pi + rows(t, nct)

    zeros = jnp.zeros((n_seg, ns), F32)
    er, ei = lax.fori_loop(0, seg, local_step, (zeros, zeros))

    pr, pi = None, None
    br_, bi_ = ar_ref[...], ai_ref[...]
    e = seg
    while e:
        if e & 1:
            pr, pi = (br_, bi_) if pr is None else _cmul(pr, pi, br_, bi_)
        e >>= 1
        if e:
            br_, bi_ = _cmul(br_, bi_, br_, bi_)

    if has_h0:
        cr, ci = h0r_ref[...], h0i_ref[...]
    else:
        cr, ci = jnp.zeros((1, ns), F32), jnp.zeros((1, ns), F32)
    row = lax.broadcasted_iota(I32, (n_seg, ns), 0)
    ir, ii = jnp.zeros((n_seg, ns), F32), jnp.zeros((n_seg, ns), F32)
    for s in range(n_seg):
        ir = jnp.where(row == s, cr, ir)
        ii = jnp.where(row == s, ci, ii)
        qr, qi = _cmul(pr, pi, cr, ci)
        cr, ci = qr + er[s:s + 1], qi + ei[s:s + 1]
    hr_ref[...] = cr
    hi_ref[...] = ci

    def full_step(t, carry):
        hr, hi = carry
        qr_, qi_ = _cmul(ar, ai, hr, hi)
        nr = qr_ + rows(t, 0)
        ni = qi_ + rows(t, nct)
        for c in range(nct):
            bu_scr[c, pl.ds(t, n_seg, stride=seg), :] = nr[:, c * LANES:(c + 1) * LANES]
            bu_scr[nct + c, pl.ds(t, n_seg, stride=seg), :] = ni[:, c * LANES:(c + 1) * LANES]
        return nr, ni

    lax.fori_loop(0, seg, full_step, (ir, ii))
    y = d_ref[...] * u
    for c in range(2 * nct):
        y = y + _bdot(bu_scr[c], c_ref[c * LANES:(c + 1) * LANES, :])
    z_ref[...] = jax.nn.gelu(y)


def _s5_scan(proj, col0, params, d, h0, n_seg=SUBLANES):
    b_blk, c_blk, a_re, a_im = params
    b, l, _ = proj.shape
    nj, _, ns2 = b_blk.shape
    ns = ns2 // 2
    has_h0 = h0 is not None
    in_specs = [pl.BlockSpec((None, l, LANES), lambda bi, j: (bi, 0, col0 + j)),
                pl.BlockSpec((None, LANES, ns2), lambda bi, j: (j, 0, 0)),
                pl.BlockSpec((None, ns2, LANES), lambda bi, j: (j, 0, 0)),
                pl.BlockSpec((None, 1, ns), lambda bi, j: (j, 0, 0)),
                pl.BlockSpec((None, 1, ns), lambda bi, j: (j, 0, 0)),
                pl.BlockSpec((1, LANES), lambda bi, j: (0, j))]
    args = [proj, b_blk, c_blk, a_re, a_im, d.reshape(1, -1)]
    st_spec = pl.BlockSpec((None, 1, ns), lambda bi, j: (bi, 0, j))
    if has_h0:
        in_specs += [st_spec, st_spec]
        args += [h0[0].reshape(b, 1, -1), h0[1].reshape(b, 1, -1)]
    return pl.pallas_call(
        functools.partial(_s5_scan_kernel, has_h0=has_h0, n_seg=n_seg),
        grid=(b, nj),
        in_specs=in_specs,
        out_specs=[pl.BlockSpec((None, l, LANES), lambda bi, j: (bi, 0, j)), st_spec, st_spec],
        out_shape=[jax.ShapeDtypeStruct((b, l, nj * LANES), F32),
                   jax.ShapeDtypeStruct((b, 1, nj * ns), F32),
                   jax.ShapeDtypeStruct((b, 1, nj * ns), F32)],
        scratch_shapes=[pltpu.VMEM((ns2 // LANES, l, LANES), F32)],
        compiler_params=_cp(("parallel", "parallel")),
        name="s5_scan",
    )(*args)


def _s5_step_kernel(u_ref, b_ref, c_ref, ar_ref, ai_ref, d_ref, h0r_ref, h0i_ref, z_ref, hr_ref, hi_ref):
    ns = ar_ref.shape[1]
    u = u_ref[...]
    bu = _bdot(u, b_ref[...])
    pr, pi = _cmul(ar_ref[...], ai_ref[...], h0r_ref[...], h0i_ref[...])
    hr = pr + bu[:, :ns]
    hi = pi + bu[:, ns:]
    hr_ref[...] = hr
    hi_ref[...] = hi
    y = _bdot(jnp.concatenate([hr, hi], axis=1), c_ref[...]) + d_ref[...] * u
    z_ref[...] = jax.nn.gelu(y)


def _s5_step(proj, col0, params, d, h0r, h0i):
    b_blk, c_blk, a_re, a_im = params
    b = proj.shape[0]
    nj, _, ns2 = b_blk.shape
    ns = ns2 // 2
    st_spec = pl.BlockSpec((b, ns), lambda j: (0, j))
    return pl.pallas_call(
        _s5_step_kernel,
        grid=(nj,),
        in_specs=[pl.BlockSpec((b, LANES), lambda j: (0, col0 + j)),
                  pl.BlockSpec((None, LANES, ns2), lambda j: (j, 0, 0)),
                  pl.BlockSpec((None, ns2, LANES), lambda j: (j, 0, 0)),
                  pl.BlockSpec((None, 1, ns), lambda j: (j, 0, 0)),
                  pl.BlockSpec((None, 1, ns), lambda j: (j, 0, 0)),
                  pl.BlockSpec((1, LANES), lambda j: (0, j)),
                  st_spec, st_spec],
        out_specs=[pl.BlockSpec((b, LANES), lambda j: (0, j)), st_spec, st_spec],
        out_shape=[jax.ShapeDtypeStruct((b, nj * LANES), F32),
                   jax.ShapeDtypeStruct((b, nj * ns), F32),
                   jax.ShapeDtypeStruct((b, nj * ns), F32)],
        compiler_params=_cp(("parallel",)),
        name="s5_step",
    )(proj, b_blk, c_blk, a_re, a_im, d.reshape(1, -1), h0r, h0i)


def _mla_in_kernel(a_ref, w_ref, qn_ref, kvn_ref, cos_ref, sin_ref, ql_ref, ckv_ref, kpe_ref, acc_scr,
                   *, q_lora, kv_lora):
    kk = pl.program_id(2)

    @pl.when(kk == 0)
    def _():
        acc_scr[...] = jnp.zeros_like(acc_scr)

    acc_scr[...] += _bdot(a_ref[...], w_ref[...])

    @pl.when(kk == pl.num_programs(2) - 1)
    def _():
        acc = acc_scr[...]
        ql_ref[...] = _rms(acc[:, :q_lora], qn_ref[...]).astype(ql_ref.dtype)
        ckv_ref[...] = _rms(acc[:, q_lora:q_lora + kv_lora], kvn_ref[...])
        t = acc[:, q_lora + kv_lora:]
        kpe_ref[...] = t * cos_ref[...] + pltpu.roll(t, LANES // 2, 1) * sin_ref[...]


def _mla_in(h, w_c, q_norm, kv_norm, cos_t, sin_t, q_lora, kv_lora, tm, tk):
    bx, lx, d = h.shape
    n = w_c.shape[1]
    if cos_t.shape[0] == 1:
        tab = pl.BlockSpec((1, LANES), lambda b, i, k: (0, 0))
    else:
        tab = pl.BlockSpec((tm, LANES), lambda b, i, k: (i, 0))
    return pl.pallas_call(
        functools.partial(_mla_in_kernel, q_lora=q_lora, kv_lora=kv_lora),
        grid=(bx, lx // tm, d // tk),
        in_specs=[pl.BlockSpec((None, tm, tk), lambda b, i, k: (b, i, k)),
                  pl.BlockSpec((tk, n), lambda b, i, k: (k, 0)),
                  pl.BlockSpec((1, q_lora), lambda b, i, k: (0, 0)),
                  pl.BlockSpec((1, kv_lora), lambda b, i, k: (0, 0)),
                  tab, tab],
        out_specs=[pl.BlockSpec((None, tm, q_lora), lambda b, i, k: (b, i, 0)),
                   pl.BlockSpec((None, tm, kv_lora), lambda b, i, k: (b, i, 0)),
                   pl.BlockSpec((None, tm, LANES), lambda b, i, k: (b, i, 0))],
        out_shape=[jax.ShapeDtypeStruct((bx, lx, q_lora), BF16),
                   jax.ShapeDtypeStruct((bx, lx, kv_lora), F32),
                   jax.ShapeDtypeStruct((bx, lx, LANES), F32)],
        scratch_shapes=[pltpu.VMEM((tm, n), F32)],
        compiler_params=_cp(("parallel", "parallel", "arbitrary")),
        name="mla_in",
    )(h, w_c, q_norm.reshape(1, -1), kv_norm.reshape(1, -1), cos_t, sin_t)


def _flash_kernel(q_ref, k_ref, v_ref, o_ref, m_scr, l_scr, acc_scr, *, tq, scale):
    i = pl.program_id(2)
    q = q_ref[...]
    m_scr[...] = jnp.full_like(m_scr, NEG_INF)
    l_scr[...] = jnp.zeros_like(l_scr)
    acc_scr[...] = jnp.zeros_like(acc_scr)

    def block(j, masked):
        start = pl.multiple_of(j * tq, tq)
        k = k_ref[pl.ds(start, tq), :]
        v = v_ref[pl.ds(start, tq), :]
        s = _bdot_nt(q, k) * scale
        if masked:
            r = lax.broadcasted_iota(I32, (tq, tq), 0)
            c = lax.broadcasted_iota(I32, (tq, tq), 1)
            s = jnp.where(c <= r, s, NEG_INF)
        m_old = m_scr[...]
        m_new = jnp.maximum(m_old, jnp.max(s, axis=-1, keepdims=True))
        alpha = jnp.exp(m_old - m_new)
        p = jnp.exp(s - m_new)
        l_scr[...] = alpha * l_scr[...] + jnp.sum(p, axis=-1, keepdims=True)
        acc_scr[...] = alpha * acc_scr[...] + _bdot(p, v)
        m_scr[...] = m_new

    def body(j, carry):
        block(j, False)
        return carry

    lax.fori_loop(0, i, body, 0)
    block(i, True)
    o_ref[...] = (acc_scr[...] / l_scr[...]).astype(o_ref.dtype)


def _flash(q, k, v, n_heads, dq, dv, tq, scale):
    b, l, _ = q.shape
    return pl.pallas_call(
        functools.partial(_flash_kernel, tq=tq, scale=scale),
        grid=(b, n_heads, l // tq),
        in_specs=[pl.BlockSpec((None, tq, dq), lambda bi, h, i: (bi, i, h)),
                  pl.BlockSpec((None, l, dq), lambda bi, h, i: (bi, 0, h)),
                  pl.BlockSpec((None, l, dv), lambda bi, h, i: (bi, 0, h))],
        out_specs=pl.BlockSpec((None, tq, dv), lambda bi, h, i: (bi, i, h)),
        out_shape=jax.ShapeDtypeStruct((b, l, n_heads * dv), BF16),
        scratch_shapes=[pltpu.VMEM((tq, 1), F32), pltpu.VMEM((tq, 1), F32), pltpu.VMEM((tq, dv), F32)],
        compiler_params=_cp(("parallel", "parallel", "arbitrary")),
        name="mla_flash",
    )(q, k, v)


def _decode_kernel(pt_ref, qa_ref, q_ref, cn_ref, kn_ref, *refs, n_pg, rope, scale):
    ckv_refs = refs[:n_pg]
    kpe_refs = refs[n_pg:2 * n_pg]
    o_ref, m_scr, l_scr, acc_scr = refs[2 * n_pg:]
    s_idx = pl.program_id(1)

    @pl.when(s_idx == 0)
    def _():
        m_scr[...] = jnp.full_like(m_scr, NEG_INF)
        l_scr[...] = jnp.zeros_like(l_scr)
        acc_scr[...] = jnp.zeros_like(acc_scr)

    qa = qa_ref[...]
    qp = q_ref[...][:, LANES:LANES + rope]
    cks = [r[...].astype(BF16) for r in ckv_refs]
    ss = [(_bdot_nt(qa, ck) + _bdot_nt(qp, kr[...])) * scale for ck, kr in zip(cks, kpe_refs)]
    m_old = m_scr[...]
    m_new = m_old
    for s in ss:
        m_new = jnp.maximum(m_new, jnp.max(s, axis=-1, keepdims=True))
    alpha = jnp.exp(m_old - m_new)
    l_new = alpha * l_scr[...]
    acc = alpha * acc_scr[...]
    for s, ck in zip(ss, cks):
        p = jnp.exp(s - m_new)
        l_new = l_new + jnp.sum(p, axis=-1, keepdims=True)
        acc = acc + _bdot(p, ck)
    m_scr[...] = m_new
    l_scr[...] = l_new
    acc_scr[...] = acc

    @pl.when(s_idx == pl.num_programs(1) - 1)
    def _():
        cn = cn_ref[...].astype(BF16).astype(F32)
        kn = kn_ref[...][:, :rope].astype(BF16).astype(F32)
        s_new = (jnp.sum(qa.astype(F32) * cn, axis=-1, keepdims=True)
                 + jnp.sum(qp.astype(F32) * kn, axis=-1, keepdims=True)) * scale
        m_f = jnp.maximum(m_new, s_new)
        a_f = jnp.exp(m_new - m_f)
        p_new = jnp.exp(s_new - m_f)
        l_f = a_f * l_new + p_new
        o = (a_f * acc) / l_f + (p_new / l_f).astype(BF16).astype(F32) * cn
        o_ref[...] = o.astype(o_ref.dtype)


def _decode(page_table, q_abs, q_s, ckv_new, kpe_new, cache_ckv, cache_kpe, layer, n_heads, rope, scale):
    db, n_pages = page_table.shape
    kv_lora = q_abs.shape[2]
    page = cache_ckv.shape[2]
    n_pg = math.gcd(n_pages, 8)
    n_steps = n_pages // n_pg
    pt_flat = page_table.reshape(-1)

    def page_spec(g, width):
        return pl.BlockSpec((None, None, page, width),
                            lambda b, s, pt: (layer, pt[b * n_pages + s * n_pg + g], 0, 0))

    in_specs = [pl.BlockSpec((None, n_heads, kv_lora), lambda b, s, pt: (b, 0, 0)),
                pl.BlockSpec((None, n_heads, 2 * LANES), lambda b, s, pt: (b, 0, 0)),
                pl.BlockSpec((None, 1, kv_lora), lambda b, s, pt: (b, 0, 0)),
                pl.BlockSpec((None, 1, LANES), lambda b, s, pt: (b, 0, 0))]
    in_specs += [page_spec(g, kv_lora) for g in range(n_pg)]
    in_specs += [page_spec(g, rope) for g in range(n_pg)]
    grid_spec = pltpu.PrefetchScalarGridSpec(
        num_scalar_prefetch=1,
        grid=(db, n_steps),
        in_specs=in_specs,
        out_specs=pl.BlockSpec((None, n_heads, kv_lora), lambda b, s, pt: (b, 0, 0)),
        scratch_shapes=[pltpu.VMEM((n_heads, 1), F32), pltpu.VMEM((n_heads, 1), F32),
                        pltpu.VMEM((n_heads, kv_lora), F32)],
    )
    return pl.pallas_call(
        functools.partial(_decode_kernel, n_pg=n_pg, rope=rope, scale=scale),
        grid_spec=grid_spec,
        out_shape=jax.ShapeDtypeStruct((db, n_heads, kv_lora), BF16),
        compiler_params=_cp(("parallel", "arbitrary")),
        name="mla_decode",
    )(pt_flat, q_abs, q_s, ckv_new, kpe_new, *([cache_ckv] * n_pg), *([cache_kpe] * n_pg))


def _gather_rows_kernel(idx_ref, src_ref, o_ref, buf, sem, *, tm):
    base = pl.program_id(0) * tm

    def row_copy(r, row):
        return pltpu.make_async_copy(src_ref.at[pl.ds(row, 1)], buf.at[pl.ds(r, 1)], sem)

    def start(r, c):
        row_copy(r, idx_ref[base + r]).start()
        return c

    lax.fori_loop(0, tm, start, 0)

    def wait(r, c):
        row_copy(r, 0).wait()
        return c

    lax.fori_loop(0, tm, wait, 0)
    o_ref[...] = buf[...].astype(o_ref.dtype)


def _gather_rows(idx, src, tm, out_dtype, name):
    n = idx.shape[0]
    d = src.shape[1]
    grid_spec = pltpu.PrefetchScalarGridSpec(
        num_scalar_prefetch=1,
        grid=(n // tm,),
        in_specs=[pl.BlockSpec(memory_space=pl.ANY)],
        out_specs=pl.BlockSpec((tm, d), lambda i, idx: (i, 0)),
        scratch_shapes=[pltpu.VMEM((tm, d), src.dtype), pltpu.SemaphoreType.DMA(())],
    )
    return pl.pallas_call(
        functools.partial(_gather_rows_kernel, tm=tm),
        grid_spec=grid_spec,
        out_shape=jax.ShapeDtypeStruct((n, d), out_dtype),
        compiler_params=_cp(("arbitrary",)),
        name=name,
    )(idx, src)


def _expert_up_kernel(te_ref, nv_ref, x_ref, wg_ref, wu_ref, wr_ref, o_ref):
    i = pl.program_id(1)

    @pl.when(i < nv_ref[0])
    def _():
        x = x_ref[...]
        act = jax.nn.silu(_bdot(x, wg_ref[...])) * _bdot(x, wu_ref[...]) * wr_ref[...]
        o_ref[...] = act.astype(o_ref.dtype)

    @pl.when(i >= nv_ref[0])
    def _():
        o_ref[...] = jnp.zeros_like(o_ref)


def _expert_down_kernel(te_ref, nv_ref, h_ref, wd_ref, o_ref):
    i = pl.program_id(1)

    @pl.when(i < nv_ref[0])
    def _():
        o_ref[...] = _bdot(h_ref[...], wd_ref[...])

    @pl.when(i >= nv_ref[0])
    def _():
        o_ref[...] = jnp.zeros_like(o_ref)


def _experts(xs, wrow, tile_expert, n_valid, w_gate, w_up, w_down, layer, tm, tf, tn):
    n, d = xs.shape
    n_tiles = n // tm
    ff = w_gate.shape[3]

    def row_tile(j, i, te, nv):
        return (jnp.minimum(i, nv[0] - 1), 0)

    up_spec = pltpu.PrefetchScalarGridSpec(
        num_scalar_prefetch=2,
        grid=(ff // tf, n_tiles),
        in_specs=[pl.BlockSpec((tm, d), row_tile),
                  pl.BlockSpec((None, None, d, tf), lambda j, i, te, nv: (layer, te[i], 0, j)),
                  pl.BlockSpec((None, None, d, tf), lambda j, i, te, nv: (layer, te[i], 0, j)),
                  pl.BlockSpec((tm, 1), row_tile)],
        out_specs=pl.BlockSpec((tm, tf), lambda j, i, te, nv: (i, j)),
    )
    act = pl.pallas_call(
        _expert_up_kernel,
        grid_spec=up_spec,
        out_shape=jax.ShapeDtypeStruct((n, ff), BF16),
        compiler_params=_cp(("parallel", "arbitrary")),
        name="expert_up",
    )(tile_expert, n_valid, xs, w_gate, w_up, wrow)
    down_spec = pltpu.PrefetchScalarGridSpec(
        num_scalar_prefetch=2,
        grid=(d // tn, n_tiles),
        in_specs=[pl.BlockSpec((tm, ff), row_tile),
                  pl.BlockSpec((None, None, ff, tn), lambda j, i, te, nv: (layer, te[i], 0, j))],
        out_specs=pl.BlockSpec((tm, tn), lambda j, i, te, nv: (i, j)),
    )
    return pl.pallas_call(
        _expert_down_kernel,
        grid_spec=down_spec,
        out_shape=jax.ShapeDtypeStruct((n, d), F32),
        compiler_params=_cp(("parallel", "arbitrary")),
        name="expert_down",
    )(tile_expert, n_valid, act, w_down)


def _dispatch_plan(e_idx, w_sel, n_exp, tm):
    t = e_idx.shape[1]
    n_tiles = (2 * t + n_exp * (tm - 1)) // tm
    e_flat = e_idx.reshape(-1)
    onehot = (e_flat[:, None] == jnp.arange(n_exp, dtype=I32)[None, :]).astype(I32)
    csum = jnp.cumsum(onehot, axis=0)
    rank = jnp.sum((csum - onehot) * onehot, axis=1)
    tiles_e = (csum[-1] + tm - 1) // tm
    tile_end = jnp.cumsum(tiles_e)
    pos = (tile_end - tiles_e)[e_flat] * tm + rank
    n_valid = tile_end[-1]
    src = jnp.zeros((n_tiles * tm,), I32).at[pos].set(jnp.tile(jnp.arange(t, dtype=I32), 2))
    wrow = jnp.zeros((n_tiles * tm,), F32).at[pos].set(w_sel.reshape(-1))
    tile_ids = jnp.arange(n_tiles, dtype=I32)
    te = jnp.minimum(jnp.searchsorted(tile_end, tile_ids, side="right"), n_exp - 1).astype(I32)
    te = jnp.where(tile_ids < n_valid, te, te[jnp.maximum(n_valid - 1, 0)])
    return src, wrow.reshape(-1, 1), te, n_valid.reshape(1).astype(I32), pos.astype(I32)


def _combine_kernel(x_ref, g_ref, c0_ref, c1_ref, o_ref):
    o_ref[...] = x_ref[...] + g_ref[...] * (c0_ref[...] + c1_ref[...])


def _combine(x, gate, contrib, row0, tl):
    bx, lx, d = x.shape
    nl = lx // tl
    off = row0 // tl
    tok = pl.BlockSpec((None, tl, d), lambda b, i: (b, i, 0))
    return pl.pallas_call(
        _combine_kernel,
        grid=(bx, nl),
        in_specs=[tok, _mod_spec(gate, tl),
                  pl.BlockSpec((None, tl, d), lambda b, i: (0, off + b * nl + i, 0)),
                  pl.BlockSpec((None, tl, d), lambda b, i: (1, off + b * nl + i, 0))],
        out_specs=tok,
        out_shape=jax.ShapeDtypeStruct((bx, lx, d), F32),
        compiler_params=_cp(("parallel", "parallel")),
        name="moe_combine",
    )(x, gate, contrib, contrib)


def _moe(f_p, f_s, route_p, route_s, w_gate, w_up, w_down, layer):
    bp, lp, d = f_p.shape
    n_p = bp * lp
    n_s = f_s.shape[1]
    n_exp = w_gate.shape[1]
    f_all = jnp.concatenate([f_p.reshape(n_p, d), f_s.reshape(n_s, d)], axis=0)
    e_idx = jnp.concatenate([route_p[0], route_s[0]], axis=1)
    w_sel = jnp.concatenate([route_p[1], route_s[1]], axis=1)
    src, wrow, te, n_valid, pos = _dispatch_plan(e_idx, w_sel, n_exp, MOE_TM)
    xs = _gather_rows(src, f_all, MOE_TM, BF16, "moe_dispatch")
    ys = _experts(xs, wrow, te, n_valid, w_gate, w_up, w_down, layer, MOE_TM,
                  min(512, w_gate.shape[3]), min(1024, d))
    t = n_p + n_s
    tg = math.gcd(2 * t, LANES)
    contrib = _gather_rows(pos, ys, tg, F32, "moe_collect")
    return contrib.reshape(2, t, d)


def kernel(x_prompt, x_sample, c_prompt, c_sample, state_ret, state_s5_re, state_s5_im, cache_ckv, cache_kpe, page_table, w_ada, b_ada, norm_mix, norm_ffn, norm_final, w_in_ab, ret_gn, s5_lam_re, s5_lam_im, s5_log_dt, s5_b_re, s5_b_im, s5_c_re, s5_c_im, s5_d, s5_w_glu, s5_b_glu, w_out_ab, w_in_c, q_norm, kv_norm, w_q_up, w_kv_up, w_out_c, w_router, b_router, w_exp_gate, w_exp_up, w_exp_down):
    bp, lp, d = x_prompt.shape
    db = x_sample.shape[0]
    assert x_sample.shape[1] == 1
    depth = w_ada.shape[0]
    past_len = page_table.shape[1] * cache_ckv.shape[2]
    _, _, ret_heads, ret_dk, ret_dv = state_ret.shape
    assert ret_dk == ret_dv and ret_dk % (2 * LANES) == 0
    ret_width = ret_heads * ret_dk
    s5_width = s5_w_glu.shape[1]
    mla_heads = w_q_up.shape[2]
    q_lora = w_q_up.shape[1]
    kv_lora = w_kv_up.shape[1]
    qk_rope = cache_kpe.shape[3]
    qk_nope = w_q_up.shape[3] - qk_rope
    v_head = w_kv_up.shape[3] - qk_nope
    assert qk_nope == LANES and v_head == LANES and qk_rope == LANES // 2
    attn_scale = (qk_nope + qk_rope) ** -0.5

    tl_p = min(256, lp)
    tm_p = min(1024, lp)
    tq = min(512, lp)

    xp = x_prompt
    xs = x_sample.reshape(1, db, d)

    mod = _adaln(jnp.concatenate([c_prompt, c_sample], axis=0), w_ada, b_ada)

    def mods(i):
        mp = [mod[i, :bp, k * d:(k + 1) * d].reshape(bp, 1, d) for k in range(6)]
        ms = [mod[i, bp:, k * d:(k + 1) * d].reshape(1, db, d) for k in range(6)]
        return mp, ms

    pos_p = jnp.arange(lp, dtype=F32)
    pos_s = past_len + jnp.arange(1, dtype=F32)
    w_router_t = w_router.T
    outs = {k: [] for k in ("ret_p", "ret_s", "s5re_p", "s5im_p", "s5re_s", "s5im_s",
                            "ckv_p", "kpe_p", "ckv_s", "kpe_s")}

    for i in range(depth):
        (sh1_p, sc1_p, g1_p, sh2_p, sc2_p, g2_p), (sh1_s, sc1_s, g1_s, sh2_s, sc2_s, g2_s) = mods(i)
        h_p = _norm_mod(xp, norm_mix[i], sh1_p, sc1_p, tl_p)
        h_s = _norm_mod(xs, norm_mix[i], sh1_s, sc1_s, db)
        j = i // 2
        if i % 2 == 0:
            w_in = (w_in_ab, j)
            tn_in = 512
            proj_p = _mm([h_p], w_in, tm=tm_p, tn=tn_in, out_dtype=F32, name="ab_in_p")
            proj_s = _mm([h_s], w_in, tm=db, tn=tn_in, out_dtype=F32, name="ab_in_s")[0]
            chunk = min(RET_CHUNK, lp)
            cos_p, sin_p = _rope_tables(pos_p, ret_dk // 2)
            ret_out_p, ret_p = _retention(proj_p, cos_p, sin_p, _ret_tables(ret_heads, chunk, chunk),
                                          ret_gn[j], None, ret_heads, ret_dk, chunk)
            cos_s, sin_s = _rope_tables(pos_s, ret_dk // 2)
            pad_rows = ((0, RET_PAD - 1), (0, 0))
            proj_s8 = jnp.pad(proj_s[:, None, :4 * ret_width], ((0, 0), (0, RET_PAD - 1), (0, 0)))
            ret_out_s8, ret_s = _retention(proj_s8, jnp.pad(cos_s, pad_rows), jnp.pad(sin_s, pad_rows),
                                           _ret_tables(ret_heads, 1, RET_PAD), ret_gn[j], state_ret[j],
                                           ret_heads, ret_dk, RET_PAD)
            ret_out_s = ret_out_s8[:, 0, :].reshape(1, db, ret_width)
            params = _s5_params(s5_lam_re[j], s5_lam_im[j], s5_log_dt[j], s5_b_re[j], s5_b_im[j],
                                s5_c_re[j], s5_c_im[j])
            col0 = 4 * ret_width // LANES
            z_p, s5re_p, s5im_p = _s5_scan(proj_p, col0, params, s5_d[j], None)
            n_grp, n_state = state_s5_re.shape[2:]
            z_s, s5re_s, s5im_s = _s5_step(proj_s, col0, params, s5_d[j],
                                           state_s5_re[j].reshape(db, -1), state_s5_im[j].reshape(db, -1))
            z_s = z_s.reshape(1, db, s5_width)

            def glu(z, tm):
                tn = min(512, s5_width)
                specs = (pl.BlockSpec((None, tm, tn), lambda b, i_, j_: (b, i_, j_)),
                         pl.BlockSpec((1, tn), lambda b, i_, j_: (0, j_)))
                return _mm([z], (s5_w_glu, j), tm=tm, tn=tn, out_dtype=BF16, epi="glu",
                           extra=(z, s5_b_glu[j].reshape(1, -1)), extra_specs=specs, name="s5_glu")

            s5_out_p = glu(z_p, tm_p)
            s5_out_s = glu(z_s, db)
            xp = _mm_resgate([ret_out_p, s5_out_p], (w_out_ab, j), xp, g1_p, tm_p, 512, "ab_out_p")
            xs = _mm_resgate([ret_out_s, s5_out_s], (w_out_ab, j), xs, g1_s, db, 512, "ab_out_s")
            outs["ret_p"].append(ret_p)
            outs["ret_s"].append(ret_s)
            outs["s5re_p"].append(s5re_p.reshape(bp, n_grp, n_state))
            outs["s5im_p"].append(s5im_p.reshape(bp, n_grp, n_state))
            outs["s5re_s"].append(s5re_s.reshape(db, n_grp, n_state))
            outs["s5im_s"].append(s5im_s.reshape(db, n_grp, n_state))
        else:
            wc = w_in_c[j]
            w_kpe = wc[:, q_lora + kv_lora:]
            half = qk_rope // 2
            w_c = jnp.concatenate([wc, -w_kpe[:, half:], w_kpe[:, :half]], axis=1)
            wq = w_q_up[j]
            wq_pe = wq[:, :, qk_nope:]
            w_q = jnp.concatenate([wq, -wq_pe[:, :, half:], wq_pe[:, :, :half]], axis=2)
            w_q = w_q.reshape(q_lora, mla_heads * 2 * LANES)
            wkv = w_kv_up[j]
            w_k = wkv[:, :, :qk_nope].reshape(kv_lora, mla_heads * qk_nope)
            w_v = wkv[:, :, qk_nope:].reshape(kv_lora, mla_heads * v_head)
            w_uk_t = wkv[:, :, :qk_nope].transpose(1, 2, 0)
            w_uv = wkv[:, :, qk_nope:].transpose(1, 0, 2)

            def pe_tables(pos):
                cos, sin = _rope_tables(pos, half)
                z = jnp.zeros((pos.shape[0], LANES - qk_rope), F32)
                return (jnp.concatenate([cos, cos, z], axis=1), jnp.concatenate([sin, sin, z], axis=1))

            def project(h, pos, tm):
                cos_t, sin_t = pe_tables(pos)
                q_lat, ckv, kpe = _mla_in(h, w_c, q_norm[j], kv_norm[j], cos_t, sin_t, q_lora, kv_lora,
                                          tm, min(512, d))
                if pos.shape[0] == 1:
                    tab = pl.BlockSpec((1, LANES), lambda b, i_, j_: (0, 0))
                else:
                    tab = pl.BlockSpec((tm, LANES), lambda b, i_, j_: (i_, 0))
                q = _mm([q_lat], w_q, tm=tm, tn=2 * LANES, out_dtype=BF16, epi="rope",
                        extra=(cos_t, sin_t), extra_specs=(tab, tab), name="mla_q_up")
                return q, ckv, kpe

            q_p, ckv_p, kpe_p = project(h_p, pos_p, tm_p)
            kspec = (pl.BlockSpec((None, tm_p, LANES), lambda b, i_, j_: (b, i_, 0)),)
            k_p = _mm([ckv_p], w_k, tm=tm_p, tn=LANES, tn_out=2 * LANES, out_dtype=BF16, epi="kcat",
                      extra=(kpe_p,), extra_specs=kspec, name="mla_k_up")
            v_p = _mm([ckv_p], w_v, tm=tm_p, tn=min(512, mla_heads * v_head), out_dtype=BF16, name="mla_v_up")
            att_p = _flash(q_p, k_p, v_p, mla_heads, 2 * LANES, v_head, tq, attn_scale)
            xp = _mm_resgate([att_p], (w_out_c, j), xp, g1_p, tm_p, 512, "c_out_p")
            q_s, ckv_s, kpe_s = project(h_s, pos_s, db)
            q_s2 = q_s.reshape(db, mla_heads * 2 * LANES)
            q_abs = _bmm_heads(q_s2, w_uk_t, lambda h_: 2 * h_, BF16, "mla_q_absorb")
            o_lat = _decode(page_table, q_abs.reshape(db, mla_heads, kv_lora),
                            q_s2.reshape(db, mla_heads, 2 * LANES), ckv_s.reshape(db, 1, kv_lora),
                            kpe_s.reshape(db, 1, LANES), cache_ckv, cache_kpe, j, mla_heads, qk_rope,
                            attn_scale)
            att_s = _bmm_heads(o_lat.reshape(db, mla_heads * kv_lora), w_uv, lambda h_: h_, BF16, "mla_v_absorb")
            xs = _mm_resgate([att_s.reshape(1, db, mla_heads * v_head)], (w_out_c, j), xs, g1_s, db, 512,
                             "c_out_s")
            outs["ckv_p"].append(ckv_p)
            outs["kpe_p"].append(kpe_p[:, :, :qk_rope])
            outs["ckv_s"].append(ckv_s.reshape(db, 1, kv_lora))
            outs["kpe_s"].append(kpe_s.reshape(db, 1, LANES)[:, :, :qk_rope])

        f_p, e_p, wsel_p = _ffn_norm_route(xp, norm_ffn[i], sh2_p, sc2_p, w_router_t, b_router, tl_p)
        f_s, e_s, wsel_s = _ffn_norm_route(xs, norm_ffn[i], sh2_s, sc2_s, w_router_t, b_router, db)
        contrib = _moe(f_p, f_s, (e_p, wsel_p), (e_s, wsel_s), w_exp_gate, w_exp_up, w_exp_down, i)
        xp = _combine(xp, g2_p, contrib, 0, tl_p)
        xs = _combine(xs, g2_s, contrib, bp * lp, db)

    y_prompt = _rmsnorm(xp, norm_final, tl_p)
    y_sample = _rmsnorm(xs, norm_final, db).reshape(db, 1, d)
    st = lambda k: jnp.stack(outs[k])
    return (y_prompt, y_sample, st("ret_p"), st("ret_s"), st("s5re_p"), st("s5im_p"), st("s5re_s"),
            st("s5im_s"), st("ckv_p"), st("kpe_p"), st("ckv_s"), st("kpe_s"))
```

```python
import functools
import math

import jax
import jax.numpy as jnp
from jax import lax
from jax.experimental import pallas as pl
from jax.experimental.pallas import tpu as pltpu

F32 = jnp.float32
BF16 = jnp.bfloat16
I32 = jnp.int32

NORM_EPS = 1e-6
GN_EPS = 1e-5
NEG_INF = -1e30
ROPE_THETA = 10000.0
RET_CHUNK = 128
N_EXPERT_GROUPS = 4
LANES = 128
SUBLANES = 8
VMEM_LIMIT = 56 * 1024 * 1024
MOE_TM = 256
RET_PAD = 16
DECODE_PAGES = 32


def _cp(sem):
    return pltpu.CompilerParams(dimension_semantics=sem, vmem_limit_bytes=VMEM_LIMIT)


def _bdot(a, b):
    return jnp.dot(a.astype(BF16), b.astype(BF16), preferred_element_type=F32)


def _bdot_nt(a, b):
    return lax.dot_general(a.astype(BF16), b.astype(BF16), (((1,), (1,)), ((), ())),
                           preferred_element_type=F32)


def _bdot_tn(a, b):
    return lax.dot_general(a.astype(BF16), b.astype(BF16), (((0,), (0,)), ((), ())),
                           preferred_element_type=F32)


def _adaln_kernel(c_ref, w_ref, b_ref, o_ref):
    o_ref[...] = _bdot(jax.nn.silu(c_ref[...]), w_ref[...]) + b_ref[...]


def _adaln(c_all, w_ada, b_ada, tn=512):
    depth, d, n = w_ada.shape
    bc = c_all.shape[0]
    return pl.pallas_call(
        _adaln_kernel,
        grid=(depth, n // tn),
        in_specs=[pl.BlockSpec((bc, d), lambda l, j: (0, 0)),
                  pl.BlockSpec((None, d, tn), lambda l, j: (l, 0, j)),
                  pl.BlockSpec((None, 1, tn), lambda l, j: (l, 0, j))],
        out_specs=pl.BlockSpec((None, bc, tn), lambda l, j: (l, 0, j)),
        out_shape=jax.ShapeDtypeStruct((depth, bc, n), F32),
        compiler_params=_cp(("parallel", "parallel")),
        name="adaln",
    )(c_all, w_ada, b_ada.reshape(depth, 1, n))


def _rms(x, g):
    return x * lax.rsqrt(jnp.mean(x * x, axis=-1, keepdims=True) + NORM_EPS) * g


def _mod_spec(m, tl):
    if m.shape[1] == 1:
        return pl.BlockSpec((None, 1, m.shape[2]), lambda b, i: (b, 0, 0))
    return pl.BlockSpec((None, tl, m.shape[2]), lambda b, i: (b, i, 0))


def _norm_mod_kernel(x_ref, g_ref, sh_ref, sc_ref, o_ref):
    y = _rms(x_ref[...], g_ref[...])
    o_ref[...] = (y * (1.0 + sc_ref[...]) + sh_ref[...]).astype(o_ref.dtype)


def _norm_mod(x, g, shift, scale, tl):
    bx, lx, d = x.shape
    return pl.pallas_call(
        _norm_mod_kernel,
        grid=(bx, lx // tl),
        in_specs=[pl.BlockSpec((None, tl, d), lambda b, i: (b, i, 0)),
                  pl.BlockSpec((1, d), lambda b, i: (0, 0)),
                  _mod_spec(shift, tl), _mod_spec(scale, tl)],
        out_specs=pl.BlockSpec((None, tl, d), lambda b, i: (b, i, 0)),
        out_shape=jax.ShapeDtypeStruct((bx, lx, d), BF16),
        compiler_params=_cp(("parallel", "parallel")),
        name="norm_mod",
    )(x, g.reshape(1, d), shift, scale)


def _rmsnorm_kernel(x_ref, g_ref, o_ref):
    o_ref[...] = _rms(x_ref[...], g_ref[...])


def _rmsnorm(x, g, tl):
    bx, lx, d = x.shape
    return pl.pallas_call(
        _rmsnorm_kernel,
        grid=(bx, lx // tl),
        in_specs=[pl.BlockSpec((None, tl, d), lambda b, i: (b, i, 0)),
                  pl.BlockSpec((1, d), lambda b, i: (0, 0))],
        out_specs=pl.BlockSpec((None, tl, d), lambda b, i: (b, i, 0)),
        out_shape=jax.ShapeDtypeStruct((bx, lx, d), F32),
        compiler_params=_cp(("parallel", "parallel")),
        name="rmsnorm",
    )(x, g.reshape(1, d))


def _first_argmax(v, iota, size):
    m = jnp.max(v, axis=1, keepdims=True)
    pos = jnp.min(jnp.where(v == m, iota, size), axis=1, keepdims=True)
    return m, pos


def _ffn_norm_route_kernel(x_ref, g_ref, sh_ref, sc_ref, wr_ref, br_ref, f_ref, e_ref, w_ref, *, n_groups):
    y = _rms(x_ref[...], g_ref[...])
    f = y * (1.0 + sc_ref[...]) + sh_ref[...]
    f_ref[...] = f
    logits = lax.dot_general(wr_ref[...], f, (((1,), (1,)), ((), ())),
                             precision=lax.Precision.HIGHEST,
                             preferred_element_type=F32)
    n_exp, tl = logits.shape
    epg = n_exp // n_groups
    scores = jax.nn.sigmoid(logits)
    sel = (scores + br_ref[...]).reshape(n_groups, epg, tl)
    sc3 = scores.reshape(n_groups, epg, tl)
    iota_e = lax.broadcasted_iota(I32, (n_groups, epg, tl), 1)
    m1, p1 = _first_argmax(sel, iota_e, epg)
    m2, p2 = _first_argmax(jnp.where(iota_e == p1, -jnp.inf, sel), iota_e, epg)
    gscore = m1 + m2
    iota_g = lax.broadcasted_iota(I32, (n_groups, 1, tl), 0)
    gmax = jnp.max(gscore, axis=0, keepdims=True)
    g_idx = jnp.min(jnp.where(gscore == gmax, iota_g, n_groups), axis=0, keepdims=True)
    sc_g, p1_g, p2_g = sc3[0:1], p1[0:1], p2[0:1]
    for gi in range(1, n_groups):
        pick = g_idx == gi
        sc_g = jnp.where(pick, sc3[gi:gi + 1], sc_g)
        p1_g = jnp.where(pick, p1[gi:gi + 1], p1_g)
        p2_g = jnp.where(pick, p2[gi:gi + 1], p2_g)
    iota_l = lax.broadcasted_iota(I32, (1, epg, tl), 1)
    s1 = jnp.sum(jnp.where(iota_l == p1_g, sc_g, 0.0), axis=1, keepdims=True)
    s2 = jnp.sum(jnp.where(iota_l == p2_g, sc_g, 0.0), axis=1, keepdims=True)
    tot = s1 + s2
    e_ref[0:1, :] = (g_idx * epg + p1_g).reshape(1, tl)
    e_ref[1:2, :] = (g_idx * epg + p2_g).reshape(1, tl)
    w_ref[0:1, :] = (s1 / tot).reshape(1, tl)
    w_ref[1:2, :] = (s2 / tot).reshape(1, tl)


def _ffn_norm_route(x, g, shift, scale, w_router_t, b_router, tl):
    bx, lx, d = x.shape
    n_exp = w_router_t.shape[0]
    nl = lx // tl
    return pl.pallas_call(
        functools.partial(_ffn_norm_route_kernel, n_groups=N_EXPERT_GROUPS),
        grid=(bx, nl),
        in_specs=[pl.BlockSpec((None, tl, d), lambda b, i: (b, i, 0)),
                  pl.BlockSpec((1, d), lambda b, i: (0, 0)),
                  _mod_spec(shift, tl), _mod_spec(scale, tl),
                  pl.BlockSpec((n_exp, d), lambda b, i: (0, 0)),
                  pl.BlockSpec((n_exp, 1), lambda b, i: (0, 0))],
        out_specs=[pl.BlockSpec((None, tl, d), lambda b, i: (b, i, 0)),
                   pl.BlockSpec((2, tl), lambda b, i: (0, b * nl + i)),
                   pl.BlockSpec((2, tl), lambda b, i: (0, b * nl + i))],
        out_shape=[jax.ShapeDtypeStruct((bx, lx, d), F32),
                   jax.ShapeDtypeStruct((2, bx * lx), I32),
                   jax.ShapeDtypeStruct((2, bx * lx), F32)],
        compiler_params=_cp(("parallel", "parallel")),
        name="ffn_norm_route",
    )(x, g.reshape(1, d), shift, scale, w_router_t, b_router.reshape(n_exp, 1))


def _mm_kernel(*refs, n_a, epi, out_scale):
    a_refs = refs[:n_a]
    w_ref = refs[n_a]
    extra = refs[n_a + 1:-1]
    o_ref = refs[-1]
    acc = None
    k0 = 0
    for a_ref in a_refs:
        kk = a_ref.shape[-1]
        part = _bdot(a_ref[...], w_ref[k0:k0 + kk, :])
        acc = part if acc is None else acc + part
        k0 += kk
    if epi == "none":
        out = acc
    elif epi == "resgate":
        res_ref, gate_ref = extra
        out = res_ref[...] + gate_ref[...] * acc
    elif epi == "glu":
        z_ref, b_ref = extra
        out = z_ref[...] * jax.nn.sigmoid(acc + b_ref[...])
    elif epi == "rope":
        cos_ref, sin_ref = extra
        parts = []
        for h in range(acc.shape[1] // (2 * LANES)):
            parts.append(acc[:, 2 * h * LANES:(2 * h + 1) * LANES])
            t = acc[:, (2 * h + 1) * LANES:(2 * h + 2) * LANES]
            parts.append(t * cos_ref[...] + pltpu.roll(t, LANES // 2, 1) * sin_ref[...])
        out = jnp.concatenate(parts, axis=1) * out_scale
    elif epi == "kcat":
        kpe_ref, = extra
        out = jnp.concatenate([acc, kpe_ref[...]], axis=1)
    o_ref[...] = out.astype(o_ref.dtype)


def _mm(a_list, w, *, tm, tn, out_dtype, epi="none", extra=(), extra_specs=(), tn_out=None, out_scale=1.0,
        name="mm"):
    bx, lx, _ = a_list[0].shape
    tn_out = tn if tn_out is None else tn_out
    in_specs = [pl.BlockSpec((None, tm, a.shape[2]), lambda b, i, j: (b, i, 0)) for a in a_list]
    if isinstance(w, tuple):
        w, layer = w
        _, k, n = w.shape
        in_specs.append(pl.BlockSpec((None, k, tn), lambda b, i, j: (layer, 0, j)))
    else:
        k, n = w.shape
        in_specs.append(pl.BlockSpec((k, tn), lambda b, i, j: (0, j)))
    n_out = n // tn * tn_out
    in_specs.extend(extra_specs)
    return pl.pallas_call(
        functools.partial(_mm_kernel, n_a=len(a_list), epi=epi, out_scale=out_scale),
        grid=(bx, lx // tm, n // tn),
        in_specs=in_specs,
        out_specs=pl.BlockSpec((None, tm, tn_out), lambda b, i, j: (b, i, j)),
        out_shape=jax.ShapeDtypeStruct((bx, lx, n_out), out_dtype),
        compiler_params=_cp(("parallel", "parallel", "arbitrary")),
        name=name,
    )(*a_list, w, *extra)


def _gate_spec(gate, tm, tn):
    if gate.shape[1] == 1:
        return pl.BlockSpec((None, 1, tn), lambda b, i, j: (b, 0, j))
    return pl.BlockSpec((None, tm, tn), lambda b, i, j: (b, i, j))


def _mm_resgate(a_list, w, res, gate, tm, tn, name):
    specs = (pl.BlockSpec((None, tm, tn), lambda b, i, j: (b, i, j)), _gate_spec(gate, tm, tn))
    return _mm(a_list, w, tm=tm, tn=tn, out_dtype=F32, epi="resgate", extra=(res, gate),
               extra_specs=specs, name=name)


def _bmm_kernel(a_ref, w_ref, o_ref):
    o_ref[...] = _bdot(a_ref[...], w_ref[...]).astype(o_ref.dtype)


def _bmm_heads(a, w, a_block_of_head, out_dtype, name):
    m = a.shape[0]
    h, ka, n = w.shape
    return pl.pallas_call(
        _bmm_kernel,
        grid=(h,),
        in_specs=[pl.BlockSpec((m, ka), lambda i: (0, a_block_of_head(i))),
                  pl.BlockSpec((None, ka, n), lambda i: (i, 0, 0))],
        out_specs=pl.BlockSpec((m, n), lambda i: (0, i)),
        out_shape=jax.ShapeDtypeStruct((m, h * n), out_dtype),
        compiler_params=_cp(("parallel",)),
        name=name,
    )(a, w)


def _rope_halves(x, cos, sin):
    half = x.shape[-1] // 2
    x1, x2 = x[:, :half], x[:, half:]
    return jnp.concatenate([x1 * cos - x2 * sin, x1 * sin + x2 * cos], axis=-1)


def _retention_kernel(*refs, has_s0, k_scale):
    if has_s0:
        (q_ref, k_ref, v_ref, g_ref, cos_ref, sin_ref, intra_ref, qdec_ref, kdec_ref, cdec_ref, gn_ref,
         s0_ref, o_ref, sfin_ref, s_scr) = refs
    else:
        (q_ref, k_ref, v_ref, g_ref, cos_ref, sin_ref, intra_ref, qdec_ref, kdec_ref, cdec_ref, gn_ref,
         o_ref, sfin_ref, s_scr) = refs
    c = pl.program_id(2)

    @pl.when(c == 0)
    def _():
        if has_s0:
            s_scr[...] = s0_ref[...]
        else:
            s_scr[...] = jnp.zeros_like(s_scr)

    cos, sin = cos_ref[...], sin_ref[...]
    hb, dk, _ = s_scr.shape
    last = c == pl.num_programs(2) - 1
    for hh in range(hb):
        cols = slice(hh * dk, (hh + 1) * dk)
        q = _rope_halves(q_ref[:, cols], cos, sin)
        k = _rope_halves(k_ref[:, cols], cos, sin) * k_scale
        v = v_ref[:, cols]
        s = s_scr[hh]
        att = _bdot_nt(q, k) * intra_ref[hh]
        o = _bdot(att, v) + _bdot(q, s) * qdec_ref[hh]
        s_new = s * cdec_ref[hh] + _bdot_tn(k * kdec_ref[hh], v)
        s_scr[hh] = s_new

        @pl.when(last)
        def _():
            sfin_ref[hh] = s_new

        mu = jnp.mean(o, axis=-1, keepdims=True)
        var = jnp.mean(jnp.square(o - mu), axis=-1, keepdims=True)
        on = (o - mu) * lax.rsqrt(var + GN_EPS) * gn_ref[:, cols]
        o_ref[:, cols] = (jax.nn.silu(g_ref[:, cols]) * on).astype(o_ref.dtype)


def _ret_tables(n_heads, chunk, pad_to):
    lg = jnp.log1p(-jnp.exp2(-5.0 - jnp.arange(n_heads, dtype=F32)))
    idx = jnp.arange(chunk, dtype=F32)
    diff = idx[:, None] - idx[None, :]
    intra = jnp.where(diff[None] >= 0, jnp.exp(jnp.maximum(diff, 0.0)[None] * lg[:, None, None]), 0.0)
    q_dec = jnp.exp((idx + 1.0)[None, :] * lg[:, None])[:, :, None]
    k_dec = jnp.exp((chunk - 1.0 - idx)[None, :] * lg[:, None])[:, :, None]
    c_dec = jnp.exp(chunk * lg)[:, None, None]
    p = pad_to - chunk
    if p:
        intra = jnp.pad(intra, ((0, 0), (0, p), (0, p)))
        q_dec = jnp.pad(q_dec, ((0, 0), (0, p), (0, 0)))
        k_dec = jnp.pad(k_dec, ((0, 0), (0, p), (0, 0)))
    return intra, q_dec, k_dec, c_dec


def _rope_tables(pos, half):
    inv = jnp.exp(-math.log(ROPE_THETA) * jnp.arange(half, dtype=F32) / half)
    ang = pos[:, None] * inv[None, :]
    return jnp.cos(ang), jnp.sin(ang)


def _retention(proj, cos, sin, tables, gn, s0, n_heads, dk, chunk, hb):
    b, l, _ = proj.shape
    nc = l // chunk
    nh = n_heads // hb
    intra, q_dec, k_dec, c_dec = tables
    has_s0 = s0 is not None

    def col(off):
        return pl.BlockSpec((None, chunk, hb * dk), lambda bi, h, c: (bi, c, off + h))

    in_specs = [col(0), col(nh), col(2 * nh), col(3 * nh),
                pl.BlockSpec((chunk, dk // 2), lambda bi, h, c: (c, 0)),
                pl.BlockSpec((chunk, dk // 2), lambda bi, h, c: (c, 0)),
                pl.BlockSpec((hb, chunk, chunk), lambda bi, h, c: (h, 0, 0)),
                pl.BlockSpec((hb, chunk, 1), lambda bi, h, c: (h, 0, 0)),
                pl.BlockSpec((hb, chunk, 1), lambda bi, h, c: (h, 0, 0)),
                pl.BlockSpec((hb, 1, 1), lambda bi, h, c: (h, 0, 0)),
                pl.BlockSpec((1, hb * dk), lambda bi, h, c: (0, h))]
    args = [proj, proj, proj, proj, cos, sin, intra, q_dec, k_dec, c_dec, gn.reshape(1, -1)]
    if has_s0:
        in_specs.append(pl.BlockSpec((None, hb, dk, dk), lambda bi, h, c: (bi, h, 0, 0)))
        args.append(s0)
    return pl.pallas_call(
        functools.partial(_retention_kernel, has_s0=has_s0, k_scale=dk ** -0.5),
        grid=(b, nh, nc),
        in_specs=in_specs,
        out_specs=[pl.BlockSpec((None, chunk, hb * dk), lambda bi, h, c: (bi, c, h)),
                   pl.BlockSpec((None, hb, dk, dk), lambda bi, h, c: (bi, h, 0, 0))],
        out_shape=[jax.ShapeDtypeStruct((b, l, n_heads * dk), BF16),
                   jax.ShapeDtypeStruct((b, n_heads, dk, dk), F32)],
        scratch_shapes=[pltpu.VMEM((hb, dk, dk), F32)],
        compiler_params=_cp(("parallel", "parallel", "arbitrary")),
        name="retention",
    )(*args)


def _cmul(ar, ai, br, bi):
    return ar * br - ai * bi, ar * bi + ai * br


def _s5_params(lam_re, lam_im, log_dt, b_re, b_im, c_re, c_im):
    g, p, gs = b_re.shape
    gpb = LANES // gs
    nj = g // gpb
    lam = lax.complex(lam_re.astype(F32), lam_im.astype(F32))
    dt = jnp.exp(log_dt.astype(F32))[:, None]
    a_bar = jnp.exp(lam * dt)
    b_bar = ((a_bar - 1.0) / lam)[:, :, None] * lax.complex(b_re.astype(F32), b_im.astype(F32))
    eye = jnp.eye(gpb, dtype=F32)

    def bdiag_in(m):
        m = m.reshape(nj, gpb, p, gs).transpose(0, 1, 3, 2)
        return (m[:, :, :, None, :] * eye[None, :, None, :, None]).reshape(nj, gpb * gs, gpb * p)

    def bdiag_out(m):
        m = m.reshape(nj, gpb, gs, p).transpose(0, 1, 3, 2)
        return (m[:, :, :, None, :] * eye[None, :, None, :, None]).reshape(nj, gpb * p, gpb * gs)

    b_blk = jnp.concatenate([bdiag_in(jnp.real(b_bar)), bdiag_in(jnp.imag(b_bar))], axis=2)
    c_blk = jnp.concatenate([bdiag_out(c_re.astype(F32)), bdiag_out(-c_im.astype(F32))], axis=1)
    a_re = jnp.real(a_bar).reshape(nj, 1, gpb * p)
    a_im = jnp.imag(a_bar).reshape(nj, 1, gpb * p)
    return b_blk, c_blk, a_re, a_im


def _s5_scan_kernel(*refs, has_h0, n_seg):
    if has_h0:
        (u_ref, b_ref, c_ref, ar_ref, ai_ref, d_ref, h0r_ref, h0i_ref, z_ref, hr_ref, hi_ref,
         up_scr, hs_scr, yp_scr) = refs
    else:
        (u_ref, b_ref, c_ref, ar_ref, ai_ref, d_ref, z_ref, hr_ref, hi_ref, up_scr, hs_scr, yp_scr) = refs
    l = u_ref.shape[0]
    ns = ar_ref.shape[1]
    seg = l // n_seg
    for s in range(n_seg):
        up_scr[pl.ds(s, seg, stride=n_seg), :] = u_ref[pl.ds(s * seg, seg), :]
    up = up_scr[...]
    hs_scr[...] = _bdot(up, b_ref[...])
    ar = jnp.broadcast_to(ar_ref[...], (n_seg, ns))
    ai = jnp.broadcast_to(ai_ref[...], (n_seg, ns))

    def rows(t):
        return hs_scr[pl.ds(pl.multiple_of(t * n_seg, n_seg), n_seg), :]

    def local_step(t, carry):
        hr, hi = carry
        pr, pi = _cmul(ar, ai, hr, hi)
        blk = rows(t)
        return pr + blk[:, :ns], pi + blk[:, ns:]

    zeros = jnp.zeros((n_seg, ns), F32)
    er, ei = lax.fori_loop(0, seg, local_step, (zeros, zeros))

    pr, pi = None, None
    br_, bi_ = ar_ref[...], ai_ref[...]
    e = seg
    while e:
        if e & 1:
            pr, pi = (br_, bi_) if pr is None else _cmul(pr, pi, br_, bi_)
        e >>= 1
        if e:
            br_, bi_ = _cmul(br_, bi_, br_, bi_)

    if has_h0:
        cr, ci = h0r_ref[...], h0i_ref[...]
    else:
        cr, ci = jnp.zeros((1, ns), F32), jnp.zeros((1, ns), F32)
    row = lax.broadcasted_iota(I32, (n_seg, ns), 0)
    ir, ii = jnp.zeros((n_seg, ns), F32), jnp.zeros((n_seg, ns), F32)
    for s in range(n_seg):
        ir = jnp.where(row == s, cr, ir)
        ii = jnp.where(row == s, ci, ii)
        qr, qi = _cmul(pr, pi, cr, ci)
        cr, ci = qr + er[s:s + 1], qi + ei[s:s + 1]
    hr_ref[...] = cr
    hi_ref[...] = ci

    def full_step(t, carry):
        hr, hi = carry
        qr_, qi_ = _cmul(ar, ai, hr, hi)
        blk = rows(t)
        nr = qr_ + blk[:, :ns]
        ni = qi_ + blk[:, ns:]
        hs_scr[pl.ds(pl.multiple_of(t * n_seg, n_seg), n_seg), :] = jnp.concatenate([nr, ni], axis=1)
        return nr, ni

    lax.fori_loop(0, seg, full_step, (ir, ii))
    yp_scr[...] = jax.nn.gelu(_bdot(hs_scr[...], c_ref[...]) + d_ref[...] * up)
    for s in range(n_seg):
        z_ref[pl.ds(s * seg, seg), :] = yp_scr[pl.ds(s, seg, stride=n_seg), :]


def _s5_scan(proj, col0, params, d, h0, n_seg=SUBLANES):
    b_blk, c_blk, a_re, a_im = params
    b, l, _ = proj.shape
    nj, _, ns2 = b_blk.shape
    ns = ns2 // 2
    has_h0 = h0 is not None
    in_specs = [pl.BlockSpec((None, l, LANES), lambda bi, j: (bi, 0, col0 + j)),
                pl.BlockSpec((None, LANES, ns2), lambda bi, j: (j, 0, 0)),
                pl.BlockSpec((None, ns2, LANES), lambda bi, j: (j, 0, 0)),
                pl.BlockSpec((None, 1, ns), lambda bi, j: (j, 0, 0)),
                pl.BlockSpec((None, 1, ns), lambda bi, j: (j, 0, 0)),
                pl.BlockSpec((1, LANES), lambda bi, j: (0, j))]
    args = [proj, b_blk, c_blk, a_re, a_im, d.reshape(1, -1)]
    st_spec = pl.BlockSpec((None, 1, ns), lambda bi, j: (bi, 0, j))
    if has_h0:
        in_specs += [st_spec, st_spec]
        args += [h0[0].reshape(b, 1, -1), h0[1].reshape(b, 1, -1)]
    return pl.pallas_call(
        functools.partial(_s5_scan_kernel, has_h0=has_h0, n_seg=n_seg),
        grid=(b, nj),
        in_specs=in_specs,
        out_specs=[pl.BlockSpec((None, l, LANES), lambda bi, j: (bi, 0, j)), st_spec, st_spec],
        out_shape=[jax.ShapeDtypeStruct((b, l, nj * LANES), F32),
                   jax.ShapeDtypeStruct((b, 1, nj * ns), F32),
                   jax.ShapeDtypeStruct((b, 1, nj * ns), F32)],
        scratch_shapes=[pltpu.VMEM((l, LANES), F32), pltpu.VMEM((l, ns2), F32), pltpu.VMEM((l, LANES), F32)],
        compiler_params=_cp(("parallel", "parallel")),
        name="s5_scan",
    )(*args)


def _s5_step_kernel(u_ref, b_ref, c_ref, ar_ref, ai_ref, d_ref, h0r_ref, h0i_ref, z_ref, hr_ref, hi_ref):
    ns = ar_ref.shape[1]
    u = u_ref[...]
    bu = _bdot(u, b_ref[...])
    pr, pi = _cmul(ar_ref[...], ai_ref[...], h0r_ref[...], h0i_ref[...])
    hr = pr + bu[:, :ns]
    hi = pi + bu[:, ns:]
    hr_ref[...] = hr
    hi_ref[...] = hi
    y = _bdot(jnp.concatenate([hr, hi], axis=1), c_ref[...]) + d_ref[...] * u
    z_ref[...] = jax.nn.gelu(y)


def _s5_step(proj, col0, params, d, h0r, h0i):
    b_blk, c_blk, a_re, a_im = params
    b = proj.shape[0]
    nj, _, ns2 = b_blk.shape
    ns = ns2 // 2
    st_spec = pl.BlockSpec((b, ns), lambda j: (0, j))
    return pl.pallas_call(
        _s5_step_kernel,
        grid=(nj,),
        in_specs=[pl.BlockSpec((b, LANES), lambda j: (0, col0 + j)),
                  pl.BlockSpec((None, LANES, ns2), lambda j: (j, 0, 0)),
                  pl.BlockSpec((None, ns2, LANES), lambda j: (j, 0, 0)),
                  pl.BlockSpec((None, 1, ns), lambda j: (j, 0, 0)),
                  pl.BlockSpec((None, 1, ns), lambda j: (j, 0, 0)),
                  pl.BlockSpec((1, LANES), lambda j: (0, j)),
                  st_spec, st_spec],
        out_specs=[pl.BlockSpec((b, LANES), lambda j: (0, j)), st_spec, st_spec],
        out_shape=[jax.ShapeDtypeStruct((b, nj * LANES), F32),
                   jax.ShapeDtypeStruct((b, nj * ns), F32),
                   jax.ShapeDtypeStruct((b, nj * ns), F32)],
        compiler_params=_cp(("parallel",)),
        name="s5_step",
    )(proj, b_blk, c_blk, a_re, a_im, d.reshape(1, -1), h0r, h0i)


def _mla_in_kernel(a_ref, w_ref, qn_ref, kvn_ref, cos_ref, sin_ref, ql_ref, ckv_ref, kpe_ref, acc_scr,
                   *, q_lora, kv_lora):
    kk = pl.program_id(2)

    @pl.when(kk == 0)
    def _():
        acc_scr[...] = jnp.zeros_like(acc_scr)

    acc_scr[...] += _bdot(a_ref[...], w_ref[...])

    @pl.when(kk == pl.num_programs(2) - 1)
    def _():
        acc = acc_scr[...]
        ql_ref[...] = _rms(acc[:, :q_lora], qn_ref[...]).astype(ql_ref.dtype)
        ckv_ref[...] = _rms(acc[:, q_lora:q_lora + kv_lora], kvn_ref[...])
        t = acc[:, q_lora + kv_lora:]
        kpe_ref[...] = t * cos_ref[...] + pltpu.roll(t, LANES // 2, 1) * sin_ref[...]


def _mla_in(h, w_c, q_norm, kv_norm, cos_t, sin_t, q_lora, kv_lora, tm, tk):
    bx, lx, d = h.shape
    n = w_c.shape[1]
    if cos_t.shape[0] == 1:
        tab = pl.BlockSpec((1, LANES), lambda b, i, k: (0, 0))
    else:
        tab = pl.BlockSpec((tm, LANES), lambda b, i, k: (i, 0))
    return pl.pallas_call(
        functools.partial(_mla_in_kernel, q_lora=q_lora, kv_lora=kv_lora),
        grid=(bx, lx // tm, d // tk),
        in_specs=[pl.BlockSpec((None, tm, tk), lambda b, i, k: (b, i, k)),
                  pl.BlockSpec((tk, n), lambda b, i, k: (k, 0)),
                  pl.BlockSpec((1, q_lora), lambda b, i, k: (0, 0)),
                  pl.BlockSpec((1, kv_lora), lambda b, i, k: (0, 0)),
                  tab, tab],
        out_specs=[pl.BlockSpec((None, tm, q_lora), lambda b, i, k: (b, i, 0)),
                   pl.BlockSpec((None, tm, kv_lora), lambda b, i, k: (b, i, 0)),
                   pl.BlockSpec((None, tm, LANES), lambda b, i, k: (b, i, 0))],
        out_shape=[jax.ShapeDtypeStruct((bx, lx, q_lora), BF16),
                   jax.ShapeDtypeStruct((bx, lx, kv_lora), F32),
                   jax.ShapeDtypeStruct((bx, lx, LANES), F32)],
        scratch_shapes=[pltpu.VMEM((tm, n), F32)],
        compiler_params=_cp(("parallel", "parallel", "arbitrary")),
        name="mla_in",
    )(h, w_c, q_norm.reshape(1, -1), kv_norm.reshape(1, -1), cos_t, sin_t)


def _flash_kernel(q_ref, k_ref, v_ref, o_ref, m_scr, l_scr, acc_scr, *, tq, hp, dq, dv):
    i = pl.program_id(2)
    m_scr[...] = jnp.full_like(m_scr, NEG_INF)
    l_scr[...] = jnp.zeros_like(l_scr)
    acc_scr[...] = jnp.zeros_like(acc_scr)

    def block(j, masked):
        start = pl.multiple_of(j * tq, tq)
        for h in range(hp):
            q = q_ref[:, h * dq:(h + 1) * dq]
            k = k_ref[pl.ds(start, tq), h * dq:(h + 1) * dq]
            v = v_ref[pl.ds(start, tq), h * dv:(h + 1) * dv]
            s = _bdot_nt(q, k)
            if masked:
                r = lax.broadcasted_iota(I32, (tq, tq), 0)
                c = lax.broadcasted_iota(I32, (tq, tq), 1)
                s = jnp.where(c <= r, s, NEG_INF)
            m_old = m_scr[h]
            m_new = jnp.maximum(m_old, jnp.max(s, axis=-1, keepdims=True))
            alpha = jnp.exp(m_old - m_new)
            p = jnp.exp(s - m_new)
            l_scr[h] = alpha * l_scr[h] + jnp.sum(p, axis=-1, keepdims=True)
            acc_scr[h] = alpha * acc_scr[h] + _bdot(p, v)
            m_scr[h] = m_new

    def body(j, carry):
        block(j, False)
        return carry

    lax.fori_loop(0, i, body, 0)
    block(i, True)
    o_ref[...] = jnp.concatenate([acc_scr[h] / l_scr[h] for h in range(hp)], axis=1).astype(o_ref.dtype)


def _flash(q, k, v, n_heads, dq, dv, tq):
    b, l, _ = q.shape
    hp = 2 if n_heads % 2 == 0 else 1
    return pl.pallas_call(
        functools.partial(_flash_kernel, tq=tq, hp=hp, dq=dq, dv=dv),
        grid=(b, n_heads // hp, l // tq),
        in_specs=[pl.BlockSpec((None, tq, hp * dq), lambda bi, h, i: (bi, i, h)),
                  pl.BlockSpec((None, l, hp * dq), lambda bi, h, i: (bi, 0, h)),
                  pl.BlockSpec((None, l, hp * dv), lambda bi, h, i: (bi, 0, h))],
        out_specs=pl.BlockSpec((None, tq, hp * dv), lambda bi, h, i: (bi, i, h)),
        out_shape=jax.ShapeDtypeStruct((b, l, n_heads * dv), BF16),
        scratch_shapes=[pltpu.VMEM((hp, tq, 1), F32), pltpu.VMEM((hp, tq, 1), F32),
                        pltpu.VMEM((hp, tq, dv), F32)],
        compiler_params=_cp(("parallel", "parallel", "arbitrary")),
        name="mla_flash",
    )(q, k, v)


def _decode_kernel(pt_ref, qa_ref, q_ref, cn_ref, kn_ref, ckv_hbm, kpe_hbm, o_ref, ckv_buf, kpe_buf, sem,
                   m_scr, l_scr, acc_scr, *, n_pg, rope, layer):
    s_idx = pl.program_id(1)
    n_steps = pl.num_programs(1)
    g = pl.program_id(0) * n_steps + s_idx
    slot = lax.rem(g, 2)

    def group_copies(gi, sl, lookup):
        out = []
        for p in range(n_pg):
            pg = pt_ref[gi * n_pg + p] if lookup else 0
            out.append(pltpu.make_async_copy(ckv_hbm.at[layer, pg], ckv_buf.at[sl, p], sem.at[sl]))
            out.append(pltpu.make_async_copy(kpe_hbm.at[layer, pg], kpe_buf.at[sl, p], sem.at[sl]))
        return out

    @pl.when(g == 0)
    def _():
        for cp in group_copies(0, 0, True):
            cp.start()

    @pl.when(g + 1 < pl.num_programs(0) * n_steps)
    def _():
        for cp in group_copies(g + 1, 1 - slot, True):
            cp.start()

    for cp in group_copies(g, slot, False):
        cp.wait()

    @pl.when(s_idx == 0)
    def _():
        m_scr[...] = jnp.full_like(m_scr, NEG_INF)
        l_scr[...] = jnp.zeros_like(l_scr)
        acc_scr[...] = jnp.zeros_like(acc_scr)

    qa = qa_ref[...]
    qp = q_ref[...][:, LANES:LANES + rope]
    cks = [ckv_buf[slot, p].astype(BF16) for p in range(n_pg)]
    ss = [_bdot_nt(qa, ck) + _bdot(qp, kpe_buf[slot, p]) for p, ck in enumerate(cks)]
    m_old = m_scr[...]
    m_new = m_old
    for s in ss:
        m_new = jnp.maximum(m_new, jnp.max(s, axis=-1, keepdims=True))
    alpha = jnp.exp(m_old - m_new)
    l_new = alpha * l_scr[...]
    acc = alpha * acc_scr[...]
    for s, ck in zip(ss, cks):
        p = jnp.exp(s - m_new)
        l_new = l_new + jnp.sum(p, axis=-1, keepdims=True)
        acc = acc + _bdot(p, ck)
    m_scr[...] = m_new
    l_scr[...] = l_new
    acc_scr[...] = acc

    @pl.when(s_idx == pl.num_programs(1) - 1)
    def _():
        cn = cn_ref[...].astype(BF16).astype(F32)
        kn = kn_ref[...][:, :rope].astype(BF16).astype(F32)
        s_new = (jnp.sum(qa.astype(F32) * cn, axis=-1, keepdims=True)
                 + jnp.sum(qp.astype(F32) * kn, axis=-1, keepdims=True))
        m_f = jnp.maximum(m_new, s_new)
        a_f = jnp.exp(m_new - m_f)
        p_new = jnp.exp(s_new - m_f)
        l_f = a_f * l_new + p_new
        o = (a_f * acc) / l_f + (p_new / l_f).astype(BF16).astype(F32) * cn
        o_ref[...] = o.astype(o_ref.dtype)


def _decode(page_table, q_abs, q_s, ckv_new, kpe_new, cache_ckv, cache_kpe_t, layer, n_heads, rope):
    db, n_pages = page_table.shape
    kv_lora = q_abs.shape[2]
    page = cache_ckv.shape[2]
    n_pg = math.gcd(n_pages, DECODE_PAGES)
    n_steps = n_pages // n_pg
    pt_flat = page_table.reshape(-1)
    in_specs = [pl.BlockSpec((None, n_heads, kv_lora), lambda b, s, pt: (b, 0, 0)),
                pl.BlockSpec((None, n_heads, 2 * LANES), lambda b, s, pt: (b, 0, 0)),
                pl.BlockSpec((None, 1, kv_lora), lambda b, s, pt: (b, 0, 0)),
                pl.BlockSpec((None, 1, LANES), lambda b, s, pt: (b, 0, 0)),
                pl.BlockSpec(memory_space=pl.ANY), pl.BlockSpec(memory_space=pl.ANY)]
    grid_spec = pltpu.PrefetchScalarGridSpec(
        num_scalar_prefetch=1,
        grid=(db, n_steps),
        in_specs=in_specs,
        out_specs=pl.BlockSpec((None, n_heads, kv_lora), lambda b, s, pt: (b, 0, 0)),
        scratch_shapes=[pltpu.VMEM((2, n_pg, page, kv_lora), cache_ckv.dtype),
                        pltpu.VMEM((2, n_pg, rope, page), cache_kpe_t.dtype),
                        pltpu.SemaphoreType.DMA((2,)),
                        pltpu.VMEM((n_heads, 1), F32), pltpu.VMEM((n_heads, 1), F32),
                        pltpu.VMEM((n_heads, kv_lora), F32)],
    )
    return pl.pallas_call(
        functools.partial(_decode_kernel, n_pg=n_pg, rope=rope, layer=layer),
        grid_spec=grid_spec,
        out_shape=jax.ShapeDtypeStruct((db, n_heads, kv_lora), BF16),
        compiler_params=_cp(("arbitrary", "arbitrary")),
        name="mla_decode",
    )(pt_flat, q_abs, q_s, ckv_new, kpe_new, cache_ckv, cache_kpe_t)


def _gather_rows_kernel(idx_ref, nv_ref, *refs, tm, n_src, split):
    srcs = refs[:n_src]
    o_ref, buf, sem = refs[n_src:]
    i = pl.program_id(0)
    nv = nv_ref[0]
    slot = lax.rem(i, 2)

    def row_copy(src, row, s, r):
        return pltpu.make_async_copy(src.at[pl.ds(row, 1)], buf.at[s, pl.ds(r, 1)], sem.at[s])

    def start_tile(t, s):
        def start(r, c):
            row = idx_ref[t * tm + r]
            if n_src == 1:
                row_copy(srcs[0], row, s, r).start()
            else:
                @pl.when(row < split)
                def _():
                    row_copy(srcs[0], row, s, r).start()

                @pl.when(row >= split)
                def _():
                    row_copy(srcs[1], row - split, s, r).start()
            return c

        lax.fori_loop(0, tm, start, 0)

    @pl.when((i == 0) & (nv > 0))
    def _():
        start_tile(0, 0)

    @pl.when(i + 1 < nv)
    def _():
        start_tile(i + 1, 1 - slot)

    @pl.when(i < nv)
    def _():
        def wait(r, c):
            row_copy(srcs[0], 0, slot, r).wait()
            return c

        lax.fori_loop(0, tm, wait, 0)
        o_ref[...] = buf[slot].astype(o_ref.dtype)

    @pl.when(i >= nv)
    def _():
        o_ref[...] = jnp.zeros_like(o_ref)


def _gather_rows(idx, n_valid, srcs, tm, out_dtype, name):
    n = idx.shape[0]
    d = srcs[0].shape[1]
    grid_spec = pltpu.PrefetchScalarGridSpec(
        num_scalar_prefetch=2,
        grid=(n // tm,),
        in_specs=[pl.BlockSpec(memory_space=pl.ANY)] * len(srcs),
        out_specs=pl.BlockSpec((tm, d), lambda i, idx, nv: (i, 0)),
        scratch_shapes=[pltpu.VMEM((2, tm, d), srcs[0].dtype), pltpu.SemaphoreType.DMA((2,))],
    )
    return pl.pallas_call(
        functools.partial(_gather_rows_kernel, tm=tm, n_src=len(srcs), split=srcs[0].shape[0]),
        grid_spec=grid_spec,
        out_shape=jax.ShapeDtypeStruct((n, d), out_dtype),
        compiler_params=_cp(("arbitrary",)),
        name=name,
    )(idx, n_valid, *srcs)


def _expert_up_kernel(te_ref, nv_ref, x_ref, wg_ref, wu_ref, wr_ref, o_ref):
    i = pl.program_id(1)

    @pl.when(i < nv_ref[0])
    def _():
        x = x_ref[...]
        act = jax.nn.silu(_bdot(x, wg_ref[...])) * _bdot(x, wu_ref[...]) * wr_ref[...]
        o_ref[...] = act.astype(o_ref.dtype)

    @pl.when(i >= nv_ref[0])
    def _():
        o_ref[...] = jnp.zeros_like(o_ref)


def _expert_down_kernel(te_ref, nv_ref, h_ref, wd_ref, o_ref):
    i = pl.program_id(1)

    @pl.when(i < nv_ref[0])
    def _():
        o_ref[...] = _bdot(h_ref[...], wd_ref[...])

    @pl.when(i >= nv_ref[0])
    def _():
        o_ref[...] = jnp.zeros_like(o_ref)


def _experts(xs, wrow, tile_expert, n_valid, w_gate, w_up, w_down, layer, tm, tf, tn):
    n, d = xs.shape
    n_tiles = n // tm
    ff = w_gate.shape[3]

    def row_tile(j, i, te, nv):
        return (jnp.minimum(i, nv[0] - 1), 0)

    up_spec = pltpu.PrefetchScalarGridSpec(
        num_scalar_prefetch=2,
        grid=(ff // tf, n_tiles),
        in_specs=[pl.BlockSpec((tm, d), row_tile),
                  pl.BlockSpec((None, None, d, tf), lambda j, i, te, nv: (layer, te[i], 0, j)),
                  pl.BlockSpec((None, None, d, tf), lambda j, i, te, nv: (layer, te[i], 0, j)),
                  pl.BlockSpec((tm, 1), row_tile)],
        out_specs=pl.BlockSpec((tm, tf), lambda j, i, te, nv: (i, j)),
    )
    act = pl.pallas_call(
        _expert_up_kernel,
        grid_spec=up_spec,
        out_shape=jax.ShapeDtypeStruct((n, ff), BF16),
        compiler_params=_cp(("parallel", "arbitrary")),
        name="expert_up",
    )(tile_expert, n_valid, xs, w_gate, w_up, wrow)
    down_spec = pltpu.PrefetchScalarGridSpec(
        num_scalar_prefetch=2,
        grid=(d // tn, n_tiles),
        in_specs=[pl.BlockSpec((tm, ff), row_tile),
                  pl.BlockSpec((None, None, ff, tn), lambda j, i, te, nv: (layer, te[i], 0, j))],
        out_specs=pl.BlockSpec((tm, tn), lambda j, i, te, nv: (i, j)),
    )
    return pl.pallas_call(
        _expert_down_kernel,
        grid_spec=down_spec,
        out_shape=jax.ShapeDtypeStruct((n, d), F32),
        compiler_params=_cp(("parallel", "arbitrary")),
        name="expert_down",
    )(tile_expert, n_valid, act, w_down)


def _dispatch_plan(e_idx, w_sel, n_exp, tm):
    t = e_idx.shape[1]
    n_tiles = (2 * t + n_exp * (tm - 1)) // tm
    e_flat = e_idx.reshape(-1)
    onehot = (e_flat[:, None] == jnp.arange(n_exp, dtype=I32)[None, :]).astype(I32)
    csum = jnp.cumsum(onehot, axis=0)
    rank = jnp.sum((csum - onehot) * onehot, axis=1)
    tiles_e = (csum[-1] + tm - 1) // tm
    tile_end = jnp.cumsum(tiles_e)
    pos = (tile_end - tiles_e)[e_flat] * tm + rank
    n_valid = tile_end[-1]
    pair = jnp.stack([jnp.tile(jnp.arange(t, dtype=I32), 2),
                      lax.bitcast_convert_type(w_sel.reshape(-1), I32)], axis=1)
    rows = jnp.zeros((n_tiles * tm, 2), I32).at[pos].set(pair)
    src = rows[:, 0]
    wrow = lax.bitcast_convert_type(rows[:, 1], F32)
    tile_ids = jnp.arange(n_tiles, dtype=I32)
    te = jnp.minimum(jnp.searchsorted(tile_end, tile_ids, side="right"), n_exp - 1).astype(I32)
    te = jnp.where(tile_ids < n_valid, te, te[jnp.maximum(n_valid - 1, 0)])
    return src, wrow.reshape(-1, 1), te, n_valid.reshape(1).astype(I32), pos.astype(I32)


def _combine_kernel(x_ref, g_ref, c0_ref, c1_ref, o_ref):
    o_ref[...] = x_ref[...] + g_ref[...] * (c0_ref[...] + c1_ref[...])


def _combine(x, gate, contrib, row0, tl):
    bx, lx, d = x.shape
    nl = lx // tl
    off = row0 // tl
    tok = pl.BlockSpec((None, tl, d), lambda b, i: (b, i, 0))
    return pl.pallas_call(
        _combine_kernel,
        grid=(bx, nl),
        in_specs=[tok, _mod_spec(gate, tl),
                  pl.BlockSpec((None, tl, d), lambda b, i: (0, off + b * nl + i, 0)),
                  pl.BlockSpec((None, tl, d), lambda b, i: (1, off + b * nl + i, 0))],
        out_specs=tok,
        out_shape=jax.ShapeDtypeStruct((bx, lx, d), F32),
        compiler_params=_cp(("parallel", "parallel")),
        name="moe_combine",
    )(x, gate, contrib, contrib)


def _moe(f_p, f_s, route_p, route_s, w_gate, w_up, w_down, layer):
    bp, lp, d = f_p.shape
    n_p = bp * lp
    n_s = f_s.shape[1]
    n_exp = w_gate.shape[1]
    e_idx = jnp.concatenate([route_p[0], route_s[0]], axis=1)
    w_sel = jnp.concatenate([route_p[1], route_s[1]], axis=1)
    src, wrow, te, n_valid, pos = _dispatch_plan(e_idx, w_sel, n_exp, MOE_TM)
    xs = _gather_rows(src, n_valid, [f_p.reshape(n_p, d), f_s.reshape(n_s, d)], MOE_TM, BF16, "moe_dispatch")
    ys = _experts(xs, wrow, te, n_valid, w_gate, w_up, w_down, layer, MOE_TM,
                  min(512, w_gate.shape[3]), min(1024, d))
    t = n_p + n_s
    tg = math.gcd(2 * t, LANES)
    contrib = _gather_rows(pos, jnp.full((1,), 2 * t // tg, I32), [ys], tg, F32, "moe_collect")
    return contrib.reshape(2, t, d)


def kernel(x_prompt, x_sample, c_prompt, c_sample, state_ret, state_s5_re, state_s5_im, cache_ckv, cache_kpe, page_table, w_ada, b_ada, norm_mix, norm_ffn, norm_final, w_in_ab, ret_gn, s5_lam_re, s5_lam_im, s5_log_dt, s5_b_re, s5_b_im, s5_c_re, s5_c_im, s5_d, s5_w_glu, s5_b_glu, w_out_ab, w_in_c, q_norm, kv_norm, w_q_up, w_kv_up, w_out_c, w_router, b_router, w_exp_gate, w_exp_up, w_exp_down):
    bp, lp, d = x_prompt.shape
    db = x_sample.shape[0]
    assert x_sample.shape[1] == 1
    depth = w_ada.shape[0]
    past_len = page_table.shape[1] * cache_ckv.shape[2]
    _, _, ret_heads, ret_dk, ret_dv = state_ret.shape
    assert ret_dk == ret_dv and ret_dk % (2 * LANES) == 0
    ret_width = ret_heads * ret_dk
    s5_width = s5_w_glu.shape[1]
    mla_heads = w_q_up.shape[2]
    q_lora = w_q_up.shape[1]
    kv_lora = w_kv_up.shape[1]
    qk_rope = cache_kpe.shape[3]
    qk_nope = w_q_up.shape[3] - qk_rope
    v_head = w_kv_up.shape[3] - qk_nope
    assert qk_nope == LANES and v_head == LANES and qk_rope == LANES // 2
    attn_scale = (qk_nope + qk_rope) ** -0.5

    tl_p = min(256, lp)
    tm_p = min(1024, lp)
    tq = min(512, lp)

    xp = x_prompt
    xs = x_sample.reshape(1, db, d)

    mod = _adaln(jnp.concatenate([c_prompt, c_sample], axis=0), w_ada, b_ada)

    def mods(i):
        mp = [mod[i, :bp, k * d:(k + 1) * d].reshape(bp, 1, d) for k in range(6)]
        ms = [mod[i, bp:, k * d:(k + 1) * d].reshape(1, db, d) for k in range(6)]
        return mp, ms

    pos_p = jnp.arange(lp, dtype=F32)
    pos_s = past_len + jnp.arange(1, dtype=F32)
    w_router_t = w_router.T
    outs = {k: [] for k in ("ret_p", "ret_s", "s5re_p", "s5im_p", "s5re_s", "s5im_s",
                            "ckv_p", "kpe_p", "ckv_s", "kpe_s")}

    for i in range(depth):
        (sh1_p, sc1_p, g1_p, sh2_p, sc2_p, g2_p), (sh1_s, sc1_s, g1_s, sh2_s, sc2_s, g2_s) = mods(i)
        h_p = _norm_mod(xp, norm_mix[i], sh1_p, sc1_p, tl_p)
        h_s = _norm_mod(xs, norm_mix[i], sh1_s, sc1_s, db)
        j = i // 2
        if i % 2 == 0:
            w_in = (w_in_ab, j)
            tn_in = 512
            proj_p = _mm([h_p], w_in, tm=tm_p, tn=tn_in, out_dtype=F32, name="ab_in_p")
            proj_s = _mm([h_s], w_in, tm=db, tn=tn_in, out_dtype=F32, name="ab_in_s")[0]
            chunk = min(RET_CHUNK, lp)
            cos_p, sin_p = _rope_tables(pos_p, ret_dk // 2)
            ret_out_p, ret_p = _retention(proj_p, cos_p, sin_p, _ret_tables(ret_heads, chunk, chunk),
                                          ret_gn[j], None, ret_heads, ret_dk, chunk, math.gcd(ret_heads, 2))
            cos_s, sin_s = _rope_tables(pos_s, ret_dk // 2)
            pad_rows = ((0, RET_PAD - 1), (0, 0))
            proj_s8 = jnp.pad(proj_s[:, None, :4 * ret_width], ((0, 0), (0, RET_PAD - 1), (0, 0)))
            ret_out_s8, ret_s = _retention(proj_s8, jnp.pad(cos_s, pad_rows), jnp.pad(sin_s, pad_rows),
                                           _ret_tables(ret_heads, 1, RET_PAD), ret_gn[j], state_ret[j],
                                           ret_heads, ret_dk, RET_PAD, ret_heads)
            ret_out_s = ret_out_s8[:, 0, :].reshape(1, db, ret_width)
            params = _s5_params(s5_lam_re[j], s5_lam_im[j], s5_log_dt[j], s5_b_re[j], s5_b_im[j],
                                s5_c_re[j], s5_c_im[j])
            col0 = 4 * ret_width // LANES
            z_p, s5re_p, s5im_p = _s5_scan(proj_p, col0, params, s5_d[j], None)
            n_grp, n_state = state_s5_re.shape[2:]
            z_s, s5re_s, s5im_s = _s5_step(proj_s, col0, params, s5_d[j],
                                           state_s5_re[j].reshape(db, -1), state_s5_im[j].reshape(db, -1))
            z_s = z_s.reshape(1, db, s5_width)

            def glu(z, tm):
                tn = min(512, s5_width)
                specs = (pl.BlockSpec((None, tm, tn), lambda b, i_, j_: (b, i_, j_)),
                         pl.BlockSpec((1, tn), lambda b, i_, j_: (0, j_)))
                return _mm([z], (s5_w_glu, j), tm=tm, tn=tn, out_dtype=BF16, epi="glu",
                           extra=(z, s5_b_glu[j].reshape(1, -1)), extra_specs=specs, name="s5_glu")

            s5_out_p = glu(z_p, tm_p)
            s5_out_s = glu(z_s, db)
            xp = _mm_resgate([ret_out_p, s5_out_p], (w_out_ab, j), xp, g1_p, tm_p, 512, "ab_out_p")
            xs = _mm_resgate([ret_out_s, s5_out_s], (w_out_ab, j), xs, g1_s, db, 512, "ab_out_s")
            outs["ret_p"].append(ret_p)
            outs["ret_s"].append(ret_s)
            outs["s5re_p"].append(s5re_p.reshape(bp, n_grp, n_state))
            outs["s5im_p"].append(s5im_p.reshape(bp, n_grp, n_state))
            outs["s5re_s"].append(s5re_s.reshape(db, n_grp, n_state))
            outs["s5im_s"].append(s5im_s.reshape(db, n_grp, n_state))
        else:
            wc = w_in_c[j]
            w_kpe = wc[:, q_lora + kv_lora:]
            half = qk_rope // 2
            w_c = jnp.concatenate([wc, -w_kpe[:, half:], w_kpe[:, :half]], axis=1)
            wq = w_q_up[j]
            wq_pe = wq[:, :, qk_nope:]
            w_q = jnp.concatenate([wq, -wq_pe[:, :, half:], wq_pe[:, :, :half]], axis=2)
            w_q = w_q.reshape(q_lora, mla_heads * 2 * LANES)
            wkv = w_kv_up[j]
            w_k = wkv[:, :, :qk_nope].reshape(kv_lora, mla_heads * qk_nope)
            w_v = wkv[:, :, qk_nope:].reshape(kv_lora, mla_heads * v_head)
            w_uk_t = wkv[:, :, :qk_nope].transpose(1, 2, 0)
            w_uv = wkv[:, :, qk_nope:].transpose(1, 0, 2)

            def pe_tables(pos):
                cos, sin = _rope_tables(pos, half)
                z = jnp.zeros((pos.shape[0], LANES - qk_rope), F32)
                return (jnp.concatenate([cos, cos, z], axis=1), jnp.concatenate([sin, sin, z], axis=1))

            def project(h, pos, tm):
                cos_t, sin_t = pe_tables(pos)
                q_lat, ckv, kpe = _mla_in(h, w_c, q_norm[j], kv_norm[j], cos_t, sin_t, q_lora, kv_lora,
                                          tm, min(512, d))
                if pos.shape[0] == 1:
                    tab = pl.BlockSpec((1, LANES), lambda b, i_, j_: (0, 0))
                else:
                    tab = pl.BlockSpec((tm, LANES), lambda b, i_, j_: (i_, 0))
                q = _mm([q_lat], w_q, tm=tm, tn=2 * LANES, out_dtype=BF16, epi="rope", out_scale=attn_scale,
                        extra=(cos_t, sin_t), extra_specs=(tab, tab), name="mla_q_up")
                return q, ckv, kpe

            q_p, ckv_p, kpe_p = project(h_p, pos_p, tm_p)
            kspec = (pl.BlockSpec((None, tm_p, LANES), lambda b, i_, j_: (b, i_, 0)),)
            k_p = _mm([ckv_p], w_k, tm=tm_p, tn=LANES, tn_out=2 * LANES, out_dtype=BF16, epi="kcat",
                      extra=(kpe_p,), extra_specs=kspec, name="mla_k_up")
            v_p = _mm([ckv_p], w_v, tm=tm_p, tn=min(512, mla_heads * v_head), out_dtype=BF16, name="mla_v_up")
            att_p = _flash(q_p, k_p, v_p, mla_heads, 2 * LANES, v_head, tq)
            xp = _mm_resgate([att_p], (w_out_c, j), xp, g1_p, tm_p, 512, "c_out_p")
            q_s, ckv_s, kpe_s = project(h_s, pos_s, db)
            q_s2 = q_s.reshape(db, mla_heads * 2 * LANES)
            q_abs = _bmm_heads(q_s2, w_uk_t, lambda h_: 2 * h_, BF16, "mla_q_absorb")
            o_lat = _decode(page_table, q_abs.reshape(db, mla_heads, kv_lora),
                            q_s2.reshape(db, mla_heads, 2 * LANES), ckv_s.reshape(db, 1, kv_lora),
                            kpe_s.reshape(db, 1, LANES), cache_ckv, jnp.swapaxes(cache_kpe, 2, 3), j, mla_heads,
                            qk_rope)
            att_s = _bmm_heads(o_lat.reshape(db, mla_heads * kv_lora), w_uv, lambda h_: h_, BF16, "mla_v_absorb")
            xs = _mm_resgate([att_s.reshape(1, db, mla_heads * v_head)], (w_out_c, j), xs, g1_s, db, 512,
                             "c_out_s")
            outs["ckv_p"].append(ckv_p)
            outs["kpe_p"].append(kpe_p[:, :, :qk_rope])
            outs["ckv_s"].append(ckv_s.reshape(db, 1, kv_lora))
            outs["kpe_s"].append(kpe_s.reshape(db, 1, LANES)[:, :, :qk_rope])

        f_p, e_p, wsel_p = _ffn_norm_route(xp, norm_ffn[i], sh2_p, sc2_p, w_router_t, b_router, tl_p)
        f_s, e_s, wsel_s = _ffn_norm_route(xs, norm_ffn[i], sh2_s, sc2_s, w_router_t, b_router, db)
        contrib = _moe(f_p, f_s, (e_p, wsel_p), (e_s, wsel_s), w_exp_gate, w_exp_up, w_exp_down, i)
        xp = _combine(xp, g2_p, contrib, 0, tl_p)
        xs = _combine(xs, g2_s, contrib, bp * lp, db)

    y_prompt = _rmsnorm(xp, norm_final, tl_p)
    y_sample = _rmsnorm(xs, norm_final, db).reshape(db, 1, d)
    st = lambda k: jnp.stack(outs[k])
    return (y_prompt, y_sample, st("ret_p"), st("ret_s"), st("s5re_p"), st("s5im_p"), st("s5re_s"),
            st("s5im_s"), st("ckv_p"), st("kpe_p"), st("ckv_s"), st("kpe_s"))
```

```python
import functools
import math

import jax
import jax.numpy as jnp
from jax import lax
from jax.experimental import pallas as pl
from jax.experimental.pallas import tpu as pltpu

F32 = jnp.float32
BF16 = jnp.bfloat16
I32 = jnp.int32

NORM_EPS = 1e-6
GN_EPS = 1e-5
NEG_INF = -1e30
ROPE_THETA = 10000.0
RET_CHUNK = 128
N_EXPERT_GROUPS = 4
LANES = 128
SUBLANES = 8
VMEM_LIMIT = 56 * 1024 * 1024
MOE_TM = 256
MOE_TF = 512
MOE_TN = 2048
GATHER_UNROLL = 8
RET_PAD = 16
DECODE_PAGES = 32


def _cp(sem):
    return pltpu.CompilerParams(dimension_semantics=sem, vmem_limit_bytes=VMEM_LIMIT)


def _bdot(a, b):
    return jnp.dot(a.astype(BF16), b.astype(BF16), preferred_element_type=F32)


def _bdot_nt(a, b):
    return lax.dot_general(a.astype(BF16), b.astype(BF16), (((1,), (1,)), ((), ())),
                           preferred_element_type=F32)


def _bdot_tn(a, b):
    return lax.dot_general(a.astype(BF16), b.astype(BF16), (((0,), (0,)), ((), ())),
                           preferred_element_type=F32)


def _adaln_kernel(c_ref, w_ref, b_ref, o_ref):
    o_ref[...] = _bdot(jax.nn.silu(c_ref[...]), w_ref[...]) + b_ref[...]


def _adaln(c_all, w_ada, b_ada, tn=512):
    depth, d, n = w_ada.shape
    bc = c_all.shape[0]
    return pl.pallas_call(
        _adaln_kernel,
        grid=(depth, n // tn),
        in_specs=[pl.BlockSpec((bc, d), lambda l, j: (0, 0)),
                  pl.BlockSpec((None, d, tn), lambda l, j: (l, 0, j)),
                  pl.BlockSpec((None, 1, tn), lambda l, j: (l, 0, j))],
        out_specs=pl.BlockSpec((None, bc, tn), lambda l, j: (l, 0, j)),
        out_shape=jax.ShapeDtypeStruct((depth, bc, n), F32),
        compiler_params=_cp(("parallel", "parallel")),
        name="adaln",
    )(c_all, w_ada, b_ada.reshape(depth, 1, n))


def _rms(x, g):
    return x * lax.rsqrt(jnp.mean(x * x, axis=-1, keepdims=True) + NORM_EPS) * g


def _mod_spec(m, tl):
    if m.shape[1] == 1:
        return pl.BlockSpec((None, 1, m.shape[2]), lambda b, i: (b, 0, 0))
    return pl.BlockSpec((None, tl, m.shape[2]), lambda b, i: (b, i, 0))


def _norm_mod_kernel(x_ref, g_ref, sh_ref, sc_ref, o_ref):
    y = _rms(x_ref[...], g_ref[...])
    o_ref[...] = (y * (1.0 + sc_ref[...]) + sh_ref[...]).astype(o_ref.dtype)


def _norm_mod(x, g, shift, scale, tl):
    bx, lx, d = x.shape
    return pl.pallas_call(
        _norm_mod_kernel,
        grid=(bx, lx // tl),
        in_specs=[pl.BlockSpec((None, tl, d), lambda b, i: (b, i, 0)),
                  pl.BlockSpec((1, d), lambda b, i: (0, 0)),
                  _mod_spec(shift, tl), _mod_spec(scale, tl)],
        out_specs=pl.BlockSpec((None, tl, d), lambda b, i: (b, i, 0)),
        out_shape=jax.ShapeDtypeStruct((bx, lx, d), BF16),
        compiler_params=_cp(("parallel", "parallel")),
        name="norm_mod",
    )(x, g.reshape(1, d), shift, scale)


def _rmsnorm_kernel(x_ref, g_ref, o_ref):
    o_ref[...] = _rms(x_ref[...], g_ref[...])


def _rmsnorm(x, g, tl):
    bx, lx, d = x.shape
    return pl.pallas_call(
        _rmsnorm_kernel,
        grid=(bx, lx // tl),
        in_specs=[pl.BlockSpec((None, tl, d), lambda b, i: (b, i, 0)),
                  pl.BlockSpec((1, d), lambda b, i: (0, 0))],
        out_specs=pl.BlockSpec((None, tl, d), lambda b, i: (b, i, 0)),
        out_shape=jax.ShapeDtypeStruct((bx, lx, d), F32),
        compiler_params=_cp(("parallel", "parallel")),
        name="rmsnorm",
    )(x, g.reshape(1, d))


def _first_argmax(v, iota, size):
    m = jnp.max(v, axis=1, keepdims=True)
    pos = jnp.min(jnp.where(v == m, iota, size), axis=1, keepdims=True)
    return m, pos


def _ffn_norm_route_kernel(x_ref, g_ref, sh_ref, sc_ref, wr_ref, br_ref, f_ref, e_ref, w_ref, *, n_groups):
    y = _rms(x_ref[...], g_ref[...])
    f = y * (1.0 + sc_ref[...]) + sh_ref[...]
    half = f.shape[1] // 2
    bits = lax.bitcast_convert_type(f.astype(BF16).astype(F32), jnp.uint32)
    f_ref[...] = bits[:, half:] | (bits[:, :half] >> 16)
    logits = lax.dot_general(wr_ref[...], f, (((1,), (1,)), ((), ())),
                             precision=lax.Precision.HIGHEST,
                             preferred_element_type=F32)
    n_exp, tl = logits.shape
    epg = n_exp // n_groups
    scores = jax.nn.sigmoid(logits)
    sel = (scores + br_ref[...]).reshape(n_groups, epg, tl)
    sc3 = scores.reshape(n_groups, epg, tl)
    iota_e = lax.broadcasted_iota(I32, (n_groups, epg, tl), 1)
    m1, p1 = _first_argmax(sel, iota_e, epg)
    m2, p2 = _first_argmax(jnp.where(iota_e == p1, -jnp.inf, sel), iota_e, epg)
    gscore = m1 + m2
    iota_g = lax.broadcasted_iota(I32, (n_groups, 1, tl), 0)
    gmax = jnp.max(gscore, axis=0, keepdims=True)
    g_idx = jnp.min(jnp.where(gscore == gmax, iota_g, n_groups), axis=0, keepdims=True)
    sc_g, p1_g, p2_g = sc3[0:1], p1[0:1], p2[0:1]
    for gi in range(1, n_groups):
        pick = g_idx == gi
        sc_g = jnp.where(pick, sc3[gi:gi + 1], sc_g)
        p1_g = jnp.where(pick, p1[gi:gi + 1], p1_g)
        p2_g = jnp.where(pick, p2[gi:gi + 1], p2_g)
    iota_l = lax.broadcasted_iota(I32, (1, epg, tl), 1)
    s1 = jnp.sum(jnp.where(iota_l == p1_g, sc_g, 0.0), axis=1, keepdims=True)
    s2 = jnp.sum(jnp.where(iota_l == p2_g, sc_g, 0.0), axis=1, keepdims=True)
    tot = s1 + s2
    e_ref[0:1, :] = (g_idx * epg + p1_g).reshape(1, tl)
    e_ref[1:2, :] = (g_idx * epg + p2_g).reshape(1, tl)
    w_ref[0:1, :] = (s1 / tot).reshape(1, tl)
    w_ref[1:2, :] = (s2 / tot).reshape(1, tl)


def _ffn_norm_route(x, g, shift, scale, w_router_t, b_router, tl):
    bx, lx, d = x.shape
    n_exp = w_router_t.shape[0]
    nl = lx // tl
    return pl.pallas_call(
        functools.partial(_ffn_norm_route_kernel, n_groups=N_EXPERT_GROUPS),
        grid=(bx, nl),
        in_specs=[pl.BlockSpec((None, tl, d), lambda b, i: (b, i, 0)),
                  pl.BlockSpec((1, d), lambda b, i: (0, 0)),
                  _mod_spec(shift, tl), _mod_spec(scale, tl),
                  pl.BlockSpec((n_exp, d), lambda b, i: (0, 0)),
                  pl.BlockSpec((n_exp, 1), lambda b, i: (0, 0))],
        out_specs=[pl.BlockSpec((None, tl, d // 2), lambda b, i: (b, i, 0)),
                   pl.BlockSpec((2, tl), lambda b, i: (0, b * nl + i)),
                   pl.BlockSpec((2, tl), lambda b, i: (0, b * nl + i))],
        out_shape=[jax.ShapeDtypeStruct((bx, lx, d // 2), jnp.uint32),
                   jax.ShapeDtypeStruct((2, bx * lx), I32),
                   jax.ShapeDtypeStruct((2, bx * lx), F32)],
        compiler_params=_cp(("parallel", "parallel")),
        name="ffn_norm_route",
    )(x, g.reshape(1, d), shift, scale, w_router_t, b_router.reshape(n_exp, 1))


def _mm_kernel(*refs, n_a, epi, out_scale):
    a_refs = refs[:n_a]
    w_ref = refs[n_a]
    extra = refs[n_a + 1:-1]
    o_ref = refs[-1]
    acc = None
    k0 = 0
    for a_ref in a_refs:
        kk = a_ref.shape[-1]
        part = _bdot(a_ref[...], w_ref[k0:k0 + kk, :])
        acc = part if acc is None else acc + part
        k0 += kk
    if epi == "none":
        out = acc
    elif epi == "resgate":
        res_ref, gate_ref = extra
        out = res_ref[...] + gate_ref[...] * acc
    elif epi == "glu":
        z_ref, b_ref = extra
        out = z_ref[...] * jax.nn.sigmoid(acc + b_ref[...])
    elif epi == "rope":
        cos_ref, sin_ref = extra
        parts = []
        for h in range(acc.shape[1] // (2 * LANES)):
            parts.append(acc[:, 2 * h * LANES:(2 * h + 1) * LANES])
            t = acc[:, (2 * h + 1) * LANES:(2 * h + 2) * LANES]
            parts.append(t * cos_ref[...] + pltpu.roll(t, LANES // 2, 1) * sin_ref[...])
        out = jnp.concatenate(parts, axis=1) * out_scale
    elif epi == "kcat":
        kpe_ref, = extra
        out = jnp.concatenate([acc, kpe_ref[...]], axis=1)
    o_ref[...] = out.astype(o_ref.dtype)


def _mm(a_list, w, *, tm, tn, out_dtype, epi="none", extra=(), extra_specs=(), tn_out=None, out_scale=1.0,
        name="mm"):
    bx, lx, _ = a_list[0].shape
    tn_out = tn if tn_out is None else tn_out
    in_specs = [pl.BlockSpec((None, tm, a.shape[2]), lambda b, i, j: (b, i, 0)) for a in a_list]
    if isinstance(w, tuple):
        w, layer = w
        _, k, n = w.shape
        in_specs.append(pl.BlockSpec((None, k, tn), lambda b, i, j: (layer, 0, j)))
    else:
        k, n = w.shape
        in_specs.append(pl.BlockSpec((k, tn), lambda b, i, j: (0, j)))
    n_out = n // tn * tn_out
    in_specs.extend(extra_specs)
    return pl.pallas_call(
        functools.partial(_mm_kernel, n_a=len(a_list), epi=epi, out_scale=out_scale),
        grid=(bx, lx // tm, n // tn),
        in_specs=in_specs,
        out_specs=pl.BlockSpec((None, tm, tn_out), lambda b, i, j: (b, i, j)),
        out_shape=jax.ShapeDtypeStruct((bx, lx, n_out), out_dtype),
        compiler_params=_cp(("parallel", "parallel", "arbitrary")),
        name=name,
    )(*a_list, w, *extra)


def _gate_spec(gate, tm, tn):
    if gate.shape[1] == 1:
        return pl.BlockSpec((None, 1, tn), lambda b, i, j: (b, 0, j))
    return pl.BlockSpec((None, tm, tn), lambda b, i, j: (b, i, j))


def _mm_resgate(a_list, w, res, gate, tm, tn, name):
    specs = (pl.BlockSpec((None, tm, tn), lambda b, i, j: (b, i, j)), _gate_spec(gate, tm, tn))
    return _mm(a_list, w, tm=tm, tn=tn, out_dtype=F32, epi="resgate", extra=(res, gate),
               extra_specs=specs, name=name)


def _bmm_kernel(a_ref, w_ref, o_ref):
    o_ref[...] = _bdot(a_ref[...], w_ref[...]).astype(o_ref.dtype)


def _bmm_heads(a, w, a_block_of_head, out_dtype, name):
    m = a.shape[0]
    h, ka, n = w.shape
    return pl.pallas_call(
        _bmm_kernel,
        grid=(h,),
        in_specs=[pl.BlockSpec((m, ka), lambda i: (0, a_block_of_head(i))),
                  pl.BlockSpec((None, ka, n), lambda i: (i, 0, 0))],
        out_specs=pl.BlockSpec((m, n), lambda i: (0, i)),
        out_shape=jax.ShapeDtypeStruct((m, h * n), out_dtype),
        compiler_params=_cp(("parallel",)),
        name=name,
    )(a, w)


def _rope_halves(x, cos, sin):
    half = x.shape[-1] // 2
    x1, x2 = x[:, :half], x[:, half:]
    return jnp.concatenate([x1 * cos - x2 * sin, x1 * sin + x2 * cos], axis=-1)


def _retention_kernel(*refs, has_s0, k_scale):
    if has_s0:
        (q_ref, k_ref, v_ref, g_ref, cos_ref, sin_ref, intra_ref, qdec_ref, kdec_ref, cdec_ref, gn_ref,
         s0_ref, o_ref, sfin_ref, s_scr) = refs
    else:
        (q_ref, k_ref, v_ref, g_ref, cos_ref, sin_ref, intra_ref, qdec_ref, kdec_ref, cdec_ref, gn_ref,
         o_ref, sfin_ref, s_scr) = refs
    c = pl.program_id(2)

    @pl.when(c == 0)
    def _():
        if has_s0:
            s_scr[...] = s0_ref[...]
        else:
            s_scr[...] = jnp.zeros_like(s_scr)

    cos, sin = cos_ref[...], sin_ref[...]
    hb, dk, _ = s_scr.shape
    last = c == pl.num_programs(2) - 1
    for hh in range(hb):
        cols = slice(hh * dk, (hh + 1) * dk)
        q = _rope_halves(q_ref[:, cols], cos, sin)
        k = _rope_halves(k_ref[:, cols], cos, sin) * k_scale
        v = v_ref[:, cols]
        s = s_scr[hh]
        att = _bdot_nt(q, k) * intra_ref[hh]
        o = _bdot(att, v) + _bdot(q, s) * qdec_ref[hh]
        s_new = s * cdec_ref[hh] + _bdot_tn(k * kdec_ref[hh], v)
        s_scr[hh] = s_new

        @pl.when(last)
        def _():
            sfin_ref[hh] = s_new

        mu = jnp.mean(o, axis=-1, keepdims=True)
        var = jnp.mean(jnp.square(o - mu), axis=-1, keepdims=True)
        on = (o - mu) * lax.rsqrt(var + GN_EPS) * gn_ref[:, cols]
        o_ref[:, cols] = (jax.nn.silu(g_ref[:, cols]) * on).astype(o_ref.dtype)


def _ret_tables(n_heads, chunk, pad_to):
    lg = jnp.log1p(-jnp.exp2(-5.0 - jnp.arange(n_heads, dtype=F32)))
    idx = jnp.arange(chunk, dtype=F32)
    diff = idx[:, None] - idx[None, :]
    intra = jnp.where(diff[None] >= 0, jnp.exp(jnp.maximum(diff, 0.0)[None] * lg[:, None, None]), 0.0)
    q_dec = jnp.exp((idx + 1.0)[None, :] * lg[:, None])[:, :, None]
    k_dec = jnp.exp((chunk - 1.0 - idx)[None, :] * lg[:, None])[:, :, None]
    c_dec = jnp.exp(chunk * lg)[:, None, None]
    p = pad_to - chunk
    if p:
        intra = jnp.pad(intra, ((0, 0), (0, p), (0, p)))
        q_dec = jnp.pad(q_dec, ((0, 0), (0, p), (0, 0)))
        k_dec = jnp.pad(k_dec, ((0, 0), (0, p), (0, 0)))
    return intra, q_dec, k_dec, c_dec


def _rope_tables(pos, half):
    inv = jnp.exp(-math.log(ROPE_THETA) * jnp.arange(half, dtype=F32) / half)
    ang = pos[:, None] * inv[None, :]
    return jnp.cos(ang), jnp.sin(ang)


def _retention(proj, cos, sin, tables, gn, s0, n_heads, dk, chunk, hb):
    b, l, _ = proj.shape
    nc = l // chunk
    nh = n_heads // hb
    intra, q_dec, k_dec, c_dec = tables
    has_s0 = s0 is not None

    def col(off):
        return pl.BlockSpec((None, chunk, hb * dk), lambda bi, h, c: (bi, c, off + h))

    in_specs = [col(0), col(nh), col(2 * nh), col(3 * nh),
                pl.BlockSpec((chunk, dk // 2), lambda bi, h, c: (c, 0)),
                pl.BlockSpec((chunk, dk // 2), lambda bi, h, c: (c, 0)),
                pl.BlockSpec((hb, chunk, chunk), lambda bi, h, c: (h, 0, 0)),
                pl.BlockSpec((hb, chunk, 1), lambda bi, h, c: (h, 0, 0)),
                pl.BlockSpec((hb, chunk, 1), lambda bi, h, c: (h, 0, 0)),
                pl.BlockSpec((hb, 1, 1), lambda bi, h, c: (h, 0, 0)),
                pl.BlockSpec((1, hb * dk), lambda bi, h, c: (0, h))]
    args = [proj, proj, proj, proj, cos, sin, intra, q_dec, k_dec, c_dec, gn.reshape(1, -1)]
    if has_s0:
        in_specs.append(pl.BlockSpec((None, hb, dk, dk), lambda bi, h, c: (bi, h, 0, 0)))
        args.append(s0)
    return pl.pallas_call(
        functools.partial(_retention_kernel, has_s0=has_s0, k_scale=dk ** -0.5),
        grid=(b, nh, nc),
        in_specs=in_specs,
        out_specs=[pl.BlockSpec((None, chunk, hb * dk), lambda bi, h, c: (bi, c, h)),
                   pl.BlockSpec((None, hb, dk, dk), lambda bi, h, c: (bi, h, 0, 0))],
        out_shape=[jax.ShapeDtypeStruct((b, l, n_heads * dk), BF16),
                   jax.ShapeDtypeStruct((b, n_heads, dk, dk), F32)],
        scratch_shapes=[pltpu.VMEM((hb, dk, dk), F32)],
        compiler_params=_cp(("parallel", "parallel", "arbitrary")),
        name="retention",
    )(*args)


def _cmul(ar, ai, br, bi):
    return ar * br - ai * bi, ar * bi + ai * br


def _s5_params(lam_re, lam_im, log_dt, b_re, b_im, c_re, c_im):
    g, p, gs = b_re.shape
    gpb = LANES // gs
    nj = g // gpb
    lam = lax.complex(lam_re.astype(F32), lam_im.astype(F32))
    dt = jnp.exp(log_dt.astype(F32))[:, None]
    a_bar = jnp.exp(lam * dt)
    b_bar = ((a_bar - 1.0) / lam)[:, :, None] * lax.complex(b_re.astype(F32), b_im.astype(F32))
    eye = jnp.eye(gpb, dtype=F32)

    def bdiag_in(m):
        m = m.reshape(nj, gpb, p, gs).transpose(0, 1, 3, 2)
        return (m[:, :, :, None, :] * eye[None, :, None, :, None]).reshape(nj, gpb * gs, gpb * p)

    def bdiag_out(m):
        m = m.reshape(nj, gpb, gs, p).transpose(0, 1, 3, 2)
        return (m[:, :, :, None, :] * eye[None, :, None, :, None]).reshape(nj, gpb * p, gpb * gs)

    b_blk = jnp.concatenate([bdiag_in(jnp.real(b_bar)), bdiag_in(jnp.imag(b_bar))], axis=2)
    c_blk = jnp.concatenate([bdiag_out(c_re.astype(F32)), bdiag_out(-c_im.astype(F32))], axis=1)
    a_re = jnp.real(a_bar).reshape(nj, 1, gpb * p)
    a_im = jnp.imag(a_bar).reshape(nj, 1, gpb * p)
    return b_blk, c_blk, a_re, a_im


def _s5_scan_kernel(*refs, has_h0, n_seg):
    if has_h0:
        (u_ref, b_ref, c_ref, ar_ref, ai_ref, d_ref, h0r_ref, h0i_ref, z_ref, hr_ref, hi_ref,
         up_scr, hs_scr, yp_scr) = refs
    else:
        (u_ref, b_ref, c_ref, ar_ref, ai_ref, d_ref, z_ref, hr_ref, hi_ref, up_scr, hs_scr, yp_scr) = refs
    l = u_ref.shape[0]
    ns = ar_ref.shape[1]
    seg = l // n_seg
    for s in range(n_seg):
        up_scr[pl.ds(s, seg, stride=n_seg), :] = u_ref[pl.ds(s * seg, seg), :]
    up = up_scr[...]
    hs_scr[...] = _bdot(up, b_ref[...])
    ar = jnp.broadcast_to(ar_ref[...], (n_seg, ns))
    ai = jnp.broadcast_to(ai_ref[...], (n_seg, ns))

    def rows(t):
        return hs_scr[pl.ds(pl.multiple_of(t * n_seg, n_seg), n_seg), :]

    def local_step(t, carry):
        hr, hi = carry
        pr, pi = _cmul(ar, ai, hr, hi)
        blk = rows(t)
        return pr + blk[:, :ns], pi + blk[:, ns:]

    zeros = jnp.zeros((n_seg, ns), F32)
    er, ei = lax.fori_loop(0, seg, local_step, (zeros, zeros))

    pr, pi = None, None
    br_, bi_ = ar_ref[...], ai_ref[...]
    e = seg
    while e:
        if e & 1:
            pr, pi = (br_, bi_) if pr is None else _cmul(pr, pi, br_, bi_)
        e >>= 1
        if e:
            br_, bi_ = _cmul(br_, bi_, br_, bi_)

    if has_h0:
        cr, ci = h0r_ref[...], h0i_ref[...]
    else:
        cr, ci = jnp.zeros((1, ns), F32), jnp.zeros((1, ns), F32)
    row = lax.broadcasted_iota(I32, (n_seg, ns), 0)
    ir, ii = jnp.zeros((n_seg, ns), F32), jnp.zeros((n_seg, ns), F32)
    for s in range(n_seg):
        ir = jnp.where(row == s, cr, ir)
        ii = jnp.where(row == s, ci, ii)
        qr, qi = _cmul(pr, pi, cr, ci)
        cr, ci = qr + er[s:s + 1], qi + ei[s:s + 1]
    hr_ref[...] = cr
    hi_ref[...] = ci

    def full_step(t, carry):
        hr, hi = carry
        qr_, qi_ = _cmul(ar, ai, hr, hi)
        blk = rows(t)
        nr = qr_ + blk[:, :ns]
        ni = qi_ + blk[:, ns:]
        hs_scr[pl.ds(pl.multiple_of(t * n_seg, n_seg), n_seg), :] = jnp.concatenate([nr, ni], axis=1)
        return nr, ni

    lax.fori_loop(0, seg, full_step, (ir, ii))
    yp_scr[...] = jax.nn.gelu(_bdot(hs_scr[...], c_ref[...]) + d_ref[...] * up)
    for s in range(n_seg):
        z_ref[pl.ds(s * seg, seg), :] = yp_scr[pl.ds(s, seg, stride=n_seg), :]


def _s5_scan(proj, col0, params, d, h0, n_seg=SUBLANES):
    b_blk, c_blk, a_re, a_im = params
    b, l, _ = proj.shape
    nj, _, ns2 = b_blk.shape
    ns = ns2 // 2
    has_h0 = h0 is not None
    in_specs = [pl.BlockSpec((None, l, LANES), lambda bi, j: (bi, 0, col0 + j)),
                pl.BlockSpec((None, LANES, ns2), lambda bi, j: (j, 0, 0)),
                pl.BlockSpec((None, ns2, LANES), lambda bi, j: (j, 0, 0)),
                pl.BlockSpec((None, 1, ns), lambda bi, j: (j, 0, 0)),
                pl.BlockSpec((None, 1, ns), lambda bi, j: (j, 0, 0)),
                pl.BlockSpec((1, LANES), lambda bi, j: (0, j))]
    args = [proj, b_blk, c_blk, a_re, a_im, d.reshape(1, -1)]
    st_spec = pl.BlockSpec((None, 1, ns), lambda bi, j: (bi, 0, j))
    if has_h0:
        in_specs += [st_spec, st_spec]
        args += [h0[0].reshape(b, 1, -1), h0[1].reshape(b, 1, -1)]
    return pl.pallas_call(
        functools.partial(_s5_scan_kernel, has_h0=has_h0, n_seg=n_seg),
        grid=(b, nj),
        in_specs=in_specs,
        out_specs=[pl.BlockSpec((None, l, LANES), lambda bi, j: (bi, 0, j)), st_spec, st_spec],
        out_shape=[jax.ShapeDtypeStruct((b, l, nj * LANES), F32),
                   jax.ShapeDtypeStruct((b, 1, nj * ns), F32),
                   jax.ShapeDtypeStruct((b, 1, nj * ns), F32)],
        scratch_shapes=[pltpu.VMEM((l, LANES), F32), pltpu.VMEM((l, ns2), F32), pltpu.VMEM((l, LANES), F32)],
        compiler_params=_cp(("parallel", "parallel")),
        name="s5_scan",
    )(*args)


def _s5_step_kernel(u_ref, b_ref, c_ref, ar_ref, ai_ref, d_ref, h0r_ref, h0i_ref, z_ref, hr_ref, hi_ref):
    ns = ar_ref.shape[1]
    u = u_ref[...]
    bu = _bdot(u, b_ref[...])
    pr, pi = _cmul(ar_ref[...], ai_ref[...], h0r_ref[...], h0i_ref[...])
    hr = pr + bu[:, :ns]
    hi = pi + bu[:, ns:]
    hr_ref[...] = hr
    hi_ref[...] = hi
    y = _bdot(jnp.concatenate([hr, hi], axis=1), c_ref[...]) + d_ref[...] * u
    z_ref[...] = jax.nn.gelu(y)


def _s5_step(proj, col0, params, d, h0r, h0i):
    b_blk, c_blk, a_re, a_im = params
    b = proj.shape[0]
    nj, _, ns2 = b_blk.shape
    ns = ns2 // 2
    st_spec = pl.BlockSpec((b, ns), lambda j: (0, j))
    return pl.pallas_call(
        _s5_step_kernel,
        grid=(nj,),
        in_specs=[pl.BlockSpec((b, LANES), lambda j: (0, col0 + j)),
                  pl.BlockSpec((None, LANES, ns2), lambda j: (j, 0, 0)),
                  pl.BlockSpec((None, ns2, LANES), lambda j: (j, 0, 0)),
                  pl.BlockSpec((None, 1, ns), lambda j: (j, 0, 0)),
                  pl.BlockSpec((None, 1, ns), lambda j: (j, 0, 0)),
                  pl.BlockSpec((1, LANES), lambda j: (0, j)),
                  st_spec, st_spec],
        out_specs=[pl.BlockSpec((b, LANES), lambda j: (0, j)), st_spec, st_spec],
        out_shape=[jax.ShapeDtypeStruct((b, nj * LANES), F32),
                   jax.ShapeDtypeStruct((b, nj * ns), F32),
                   jax.ShapeDtypeStruct((b, nj * ns), F32)],
        compiler_params=_cp(("parallel",)),
        name="s5_step",
    )(proj, b_blk, c_blk, a_re, a_im, d.reshape(1, -1), h0r, h0i)


def _mla_in_kernel(a_ref, w_ref, qn_ref, kvn_ref, cos_ref, sin_ref, ql_ref, ckv_ref, kpe_ref, acc_scr,
                   *, q_lora, kv_lora):
    kk = pl.program_id(2)

    @pl.when(kk == 0)
    def _():
        acc_scr[...] = jnp.zeros_like(acc_scr)

    acc_scr[...] += _bdot(a_ref[...], w_ref[...])

    @pl.when(kk == pl.num_programs(2) - 1)
    def _():
        acc = acc_scr[...]
        ql_ref[...] = _rms(acc[:, :q_lora], qn_ref[...]).astype(ql_ref.dtype)
        ckv_ref[...] = _rms(acc[:, q_lora:q_lora + kv_lora], kvn_ref[...])
        t = acc[:, q_lora + kv_lora:]
        kpe_ref[...] = t * cos_ref[...] + pltpu.roll(t, LANES // 2, 1) * sin_ref[...]


def _mla_in(h, w_c, q_norm, kv_norm, cos_t, sin_t, q_lora, kv_lora, tm, tk):
    bx, lx, d = h.shape
    n = w_c.shape[1]
    if cos_t.shape[0] == 1:
        tab = pl.BlockSpec((1, LANES), lambda b, i, k: (0, 0))
    else:
        tab = pl.BlockSpec((tm, LANES), lambda b, i, k: (i, 0))
    return pl.pallas_call(
        functools.partial(_mla_in_kernel, q_lora=q_lora, kv_lora=kv_lora),
        grid=(bx, lx // tm, d // tk),
        in_specs=[pl.BlockSpec((None, tm, tk), lambda b, i, k: (b, i, k)),
                  pl.BlockSpec((tk, n), lambda b, i, k: (k, 0)),
                  pl.BlockSpec((1, q_lora), lambda b, i, k: (0, 0)),
                  pl.BlockSpec((1, kv_lora), lambda b, i, k: (0, 0)),
                  tab, tab],
        out_specs=[pl.BlockSpec((None, tm, q_lora), lambda b, i, k: (b, i, 0)),
                   pl.BlockSpec((None, tm, kv_lora), lambda b, i, k: (b, i, 0)),
                   pl.BlockSpec((None, tm, LANES), lambda b, i, k: (b, i, 0))],
        out_shape=[jax.ShapeDtypeStruct((bx, lx, q_lora), BF16),
                   jax.ShapeDtypeStruct((bx, lx, kv_lora), F32),
                   jax.ShapeDtypeStruct((bx, lx, LANES), F32)],
        scratch_shapes=[pltpu.VMEM((tm, n), F32)],
        compiler_params=_cp(("parallel", "parallel", "arbitrary")),
        name="mla_in",
    )(h, w_c, q_norm.reshape(1, -1), kv_norm.reshape(1, -1), cos_t, sin_t)


def _vt_up_kernel(w_ref, c_ref, o_ref):
    o_ref[...] = _bdot_nt(w_ref[...], c_ref[...]).astype(o_ref.dtype)


def _vt_up(ckv, w_vt, tl):
    b, l, r = ckv.shape
    h, dv, _ = w_vt.shape
    return pl.pallas_call(
        _vt_up_kernel,
        grid=(b, l // tl, h),
        in_specs=[pl.BlockSpec((None, dv, r), lambda bi, i, hh: (hh, 0, 0)),
                  pl.BlockSpec((None, tl, r), lambda bi, i, hh: (bi, i, 0))],
        out_specs=pl.BlockSpec((None, dv, tl), lambda bi, i, hh: (bi, hh, i)),
        out_shape=jax.ShapeDtypeStruct((b, h * dv, l), BF16),
        compiler_params=_cp(("parallel", "parallel", "arbitrary")),
        name="mla_vt_up",
    )(w_vt, ckv)


def _flash_kernel(q_ref, k_ref, v_ref, o_ref, m_scr, l_scr, acc_scr, *, tq, hp, dq, dv):
    i = pl.program_id(2)
    m_scr[...] = jnp.full_like(m_scr, NEG_INF)
    l_scr[...] = jnp.zeros_like(l_scr)
    acc_scr[...] = jnp.zeros_like(acc_scr)

    def block(j, masked):
        start = pl.multiple_of(j * tq, tq)
        for h in range(hp):
            q = q_ref[:, h * dq:(h + 1) * dq]
            k = k_ref[pl.ds(start, tq), h * dq:(h + 1) * dq]
            vt = v_ref[h * dv:(h + 1) * dv, pl.ds(start, tq)]
            st = _bdot_nt(k, q)
            if masked:
                kr = lax.broadcasted_iota(I32, (tq, tq), 0)
                qc = lax.broadcasted_iota(I32, (tq, tq), 1)
                st = jnp.where(kr <= qc, st, NEG_INF)
            m_old = m_scr[h]
            m_new = jnp.maximum(m_old, jnp.max(st, axis=0, keepdims=True))
            alpha = jnp.exp(m_old - m_new)
            p = jnp.exp(st - m_new)
            l_scr[h] = alpha * l_scr[h] + jnp.sum(p, axis=0, keepdims=True)
            acc_scr[h] = alpha * acc_scr[h] + _bdot(vt, p)
            m_scr[h] = m_new

    def body(j, carry):
        block(j, False)
        return carry

    lax.fori_loop(0, i, body, 0)
    block(i, True)
    for h in range(hp):
        o_ref[:, h * dv:(h + 1) * dv] = (acc_scr[h] / l_scr[h]).T.astype(o_ref.dtype)


def _flash(q, k, vt, n_heads, dq, dv, tq):
    b, l, _ = q.shape
    hp = 2 if n_heads % 2 == 0 else 1
    return pl.pallas_call(
        functools.partial(_flash_kernel, tq=tq, hp=hp, dq=dq, dv=dv),
        grid=(b, n_heads // hp, l // tq),
        in_specs=[pl.BlockSpec((None, tq, hp * dq), lambda bi, h, i: (bi, i, h)),
                  pl.BlockSpec((None, l, hp * dq), lambda bi, h, i: (bi, 0, h)),
                  pl.BlockSpec((None, hp * dv, l), lambda bi, h, i: (bi, h, 0))],
        out_specs=pl.BlockSpec((None, tq, hp * dv), lambda bi, h, i: (bi, i, h)),
        out_shape=jax.ShapeDtypeStruct((b, l, n_heads * dv), BF16),
        scratch_shapes=[pltpu.VMEM((hp, 1, tq), F32), pltpu.VMEM((hp, 1, tq), F32),
                        pltpu.VMEM((hp, dv, tq), F32)],
        compiler_params=_cp(("parallel", "parallel", "arbitrary")),
        name="mla_flash",
    )(q, k, vt)


def _decode_kernel(pt_ref, qa_ref, q_ref, cn_ref, kn_ref, ckv_hbm, kpe_hbm, o_ref, ckv_buf, kpe_buf, sem,
                   m_scr, l_scr, acc_scr, *, n_pg, rope, layer):
    s_idx = pl.program_id(1)
    n_steps = pl.num_programs(1)
    g = pl.program_id(0) * n_steps + s_idx
    slot = lax.rem(g, 2)

    def group_copies(gi, sl, lookup):
        out = []
        for p in range(n_pg):
            pg = pt_ref[gi * n_pg + p] if lookup else 0
            out.append(pltpu.make_async_copy(ckv_hbm.at[layer, pg], ckv_buf.at[sl, p], sem.at[sl]))
            out.append(pltpu.make_async_copy(kpe_hbm.at[layer, pg], kpe_buf.at[sl, p], sem.at[sl]))
        return out

    @pl.when(g == 0)
    def _():
        for cp in group_copies(0, 0, True):
            cp.start()

    @pl.when(g + 1 < pl.num_programs(0) * n_steps)
    def _():
        for cp in group_copies(g + 1, 1 - slot, True):
            cp.start()

    for cp in group_copies(g, slot, False):
        cp.wait()

    @pl.when(s_idx == 0)
    def _():
        m_scr[...] = jnp.full_like(m_scr, NEG_INF)
        l_scr[...] = jnp.zeros_like(l_scr)
        acc_scr[...] = jnp.zeros_like(acc_scr)

    qa = qa_ref[...]
    qp = q_ref[...][:, LANES:LANES + rope]
    cks = [ckv_buf[slot, p].astype(BF16) for p in range(n_pg)]
    ss = [_bdot_nt(qa, ck) + _bdot(qp, kpe_buf[slot, p]) for p, ck in enumerate(cks)]
    m_old = m_scr[...]
    m_new = m_old
    for s in ss:
        m_new = jnp.maximum(m_new, jnp.max(s, axis=-1, keepdims=True))
    alpha = jnp.exp(m_old - m_new)
    l_new = alpha * l_scr[...]
    acc = alpha * acc_scr[...]
    for s, ck in zip(ss, cks):
        p = jnp.exp(s - m_new)
        l_new = l_new + jnp.sum(p, axis=-1, keepdims=True)
        acc = acc + _bdot(p, ck)
    m_scr[...] = m_new
    l_scr[...] = l_new
    acc_scr[...] = acc

    @pl.when(s_idx == pl.num_programs(1) - 1)
    def _():
        cn = cn_ref[...].astype(BF16).astype(F32)
        kn = kn_ref[...][:, :rope].astype(BF16).astype(F32)
        s_new = (jnp.sum(qa.astype(F32) * cn, axis=-1, keepdims=True)
                 + jnp.sum(qp.astype(F32) * kn, axis=-1, keepdims=True))
        m_f = jnp.maximum(m_new, s_new)
        a_f = jnp.exp(m_new - m_f)
        p_new = jnp.exp(s_new - m_f)
        l_f = a_f * l_new + p_new
        o = (a_f * acc) / l_f + (p_new / l_f).astype(BF16).astype(F32) * cn
        o_ref[...] = o.astype(o_ref.dtype)


def _decode(page_table, q_abs, q_s, ckv_new, kpe_new, cache_ckv, cache_kpe_t, layer, n_heads, rope):
    db, n_pages = page_table.shape
    kv_lora = q_abs.shape[2]
    page = cache_ckv.shape[2]
    n_pg = math.gcd(n_pages, DECODE_PAGES)
    n_steps = n_pages // n_pg
    pt_flat = page_table.reshape(-1)
    in_specs = [pl.BlockSpec((None, n_heads, kv_lora), lambda b, s, pt: (b, 0, 0)),
                pl.BlockSpec((None, n_heads, 2 * LANES), lambda b, s, pt: (b, 0, 0)),
                pl.BlockSpec((None, 1, kv_lora), lambda b, s, pt: (b, 0, 0)),
                pl.BlockSpec((None, 1, LANES), lambda b, s, pt: (b, 0, 0)),
                pl.BlockSpec(memory_space=pl.ANY), pl.BlockSpec(memory_space=pl.ANY)]
    grid_spec = pltpu.PrefetchScalarGridSpec(
        num_scalar_prefetch=1,
        grid=(db, n_steps),
        in_specs=in_specs,
        out_specs=pl.BlockSpec((None, n_heads, kv_lora), lambda b, s, pt: (b, 0, 0)),
        scratch_shapes=[pltpu.VMEM((2, n_pg, page, kv_lora), cache_ckv.dtype),
                        pltpu.VMEM((2, n_pg, rope, page), cache_kpe_t.dtype),
                        pltpu.SemaphoreType.DMA((2,)),
                        pltpu.VMEM((n_heads, 1), F32), pltpu.VMEM((n_heads, 1), F32),
                        pltpu.VMEM((n_heads, kv_lora), F32)],
    )
    return pl.pallas_call(
        functools.partial(_decode_kernel, n_pg=n_pg, rope=rope, layer=layer),
        grid_spec=grid_spec,
        out_shape=jax.ShapeDtypeStruct((db, n_heads, kv_lora), BF16),
        compiler_params=_cp(("arbitrary", "arbitrary")),
        name="mla_decode",
    )(pt_flat, q_abs, q_s, ckv_new, kpe_new, cache_ckv, cache_kpe_t)


def _gather_rows_kernel(idx_ref, nv_ref, *refs, tm, n_src, split):
    srcs = refs[:n_src]
    o_ref, buf, sem = refs[n_src:]
    i = pl.program_id(0)
    nv = nv_ref[0]
    slot = lax.rem(i, 2)

    def row_copy(src, row, s, r):
        return pltpu.make_async_copy(src.at[pl.ds(row, 1)], buf.at[s, pl.ds(r, 1)], sem.at[s])

    def start_tile(t, s):
        def start(r, c):
            row = idx_ref[t * tm + r]
            if n_src == 1:
                row_copy(srcs[0], row, s, r).start()
            else:
                @pl.when(row < split)
                def _():
                    row_copy(srcs[0], row, s, r).start()

                @pl.when(row >= split)
                def _():
                    row_copy(srcs[1], row - split, s, r).start()
            return c

        lax.fori_loop(0, tm, start, 0, unroll=GATHER_UNROLL)

    @pl.when((i == 0) & (nv > 0))
    def _():
        start_tile(0, 0)

    @pl.when(i + 1 < nv)
    def _():
        start_tile(i + 1, 1 - slot)

    @pl.when(i < nv)
    def _():
        def wait(r, c):
            row_copy(srcs[0], 0, slot, r).wait()
            return c

        lax.fori_loop(0, tm, wait, 0, unroll=GATHER_UNROLL)
        o_ref[...] = buf[slot].astype(o_ref.dtype)

    @pl.when(i >= nv)
    def _():
        o_ref[...] = jnp.zeros_like(o_ref)


def _gather_rows(idx, n_valid, srcs, tm, out_dtype, name):
    n = idx.shape[0]
    d = srcs[0].shape[1]
    grid_spec = pltpu.PrefetchScalarGridSpec(
        num_scalar_prefetch=2,
        grid=(n // tm,),
        in_specs=[pl.BlockSpec(memory_space=pl.ANY)] * len(srcs),
        out_specs=pl.BlockSpec((tm, d), lambda i, idx, nv: (i, 0)),
        scratch_shapes=[pltpu.VMEM((2, tm, d), srcs[0].dtype), pltpu.SemaphoreType.DMA((2,))],
    )
    return pl.pallas_call(
        functools.partial(_gather_rows_kernel, tm=tm, n_src=len(srcs), split=srcs[0].shape[0]),
        grid_spec=grid_spec,
        out_shape=jax.ShapeDtypeStruct((n, d), out_dtype),
        compiler_params=_cp(("arbitrary",)),
        name=name,
    )(idx, n_valid, *srcs)


def _run_weights(plan_refs, w_hbms, bufs, sem, layer, tw):
    nv_ref, first_ref, run_ref, rune_ref, nr_ref = plan_refs
    j, i = pl.program_id(0), pl.program_id(1)
    nr = nr_ref[0]
    valid = i < nv_ref[0]
    r = run_ref[i]
    slot = lax.rem(j * nr + r, 2)

    def fetch(e, jj, sl):
        cols = pl.ds(pl.multiple_of(jj * tw, tw), tw)
        return [pltpu.make_async_copy(w.at[layer, e, :, cols], b.at[sl], sem.at[sl]) for w, b in zip(w_hbms, bufs)]

    @pl.when(valid & (first_ref[i] == 1))
    def _():
        @pl.when((j == 0) & (r == 0))
        def _():
            for cp in fetch(rune_ref[0], 0, 0):
                cp.start()

        wraps = r + 1 == nr
        rn = jnp.where(wraps, 0, r + 1)
        jn = jnp.where(wraps, j + 1, j)

        @pl.when(jn < pl.num_programs(0))
        def _():
            for cp in fetch(rune_ref[rn], jn, 1 - slot):
                cp.start()

        for cp in fetch(0, 0, slot):
            cp.wait()

    return valid, slot


def _expert_up_kernel(nv_ref, first_ref, run_ref, rune_ref, nr_ref, x_ref, wr_ref, wg_hbm, wu_hbm, o_ref,
                      gbuf, ubuf, sem, *, layer, tf):
    valid, slot = _run_weights((nv_ref, first_ref, run_ref, rune_ref, nr_ref), (wg_hbm, wu_hbm), (gbuf, ubuf),
                               sem, layer, tf)

    @pl.when(valid)
    def _():
        packed = x_ref[...]
        lo = lax.bitcast_convert_type(packed << 16, F32)
        hi = lax.bitcast_convert_type(packed & jnp.uint32(0xFFFF0000), F32)
        x = jnp.concatenate([lo, hi], axis=1).astype(BF16)
        act = jax.nn.silu(_bdot(x, gbuf[slot])) * _bdot(x, ubuf[slot]) * wr_ref[...]
        o_ref[...] = act.astype(o_ref.dtype)

    @pl.when(jnp.logical_not(valid))
    def _():
        o_ref[...] = jnp.zeros_like(o_ref)


def _expert_down_kernel(nv_ref, first_ref, run_ref, rune_ref, nr_ref, h_ref, wd_hbm, o_ref, dbuf, sem,
                        *, layer, tn):
    valid, slot = _run_weights((nv_ref, first_ref, run_ref, rune_ref, nr_ref), (wd_hbm,), (dbuf,), sem, layer, tn)

    @pl.when(valid)
    def _():
        o_ref[...] = _bdot(h_ref[...], dbuf[slot])

    @pl.when(jnp.logical_not(valid))
    def _():
        o_ref[...] = jnp.zeros_like(o_ref)


def _experts(xs, wrow, plan, w_gate, w_up, w_down, layer, tm, tf, tn):
    n, dh = xs.shape
    d = 2 * dh
    n_tiles = n // tm
    ff = w_gate.shape[3]
    hbm = pl.BlockSpec(memory_space=pl.ANY)

    def row_tile(j, i, nv, *_):
        return (jnp.minimum(i, nv[0] - 1), 0)

    up_spec = pltpu.PrefetchScalarGridSpec(
        num_scalar_prefetch=5,
        grid=(ff // tf, n_tiles),
        in_specs=[pl.BlockSpec((tm, dh), row_tile), pl.BlockSpec((tm, 1), row_tile), hbm, hbm],
        out_specs=pl.BlockSpec((tm, tf), lambda j, i, *_: (i, j)),
        scratch_shapes=[pltpu.VMEM((2, d, tf), w_gate.dtype), pltpu.VMEM((2, d, tf), w_up.dtype),
                        pltpu.SemaphoreType.DMA((2,))],
    )
    act = pl.pallas_call(
        functools.partial(_expert_up_kernel, layer=layer, tf=tf),
        grid_spec=up_spec,
        out_shape=jax.ShapeDtypeStruct((n, ff), BF16),
        compiler_params=_cp(("arbitrary", "arbitrary")),
        name="expert_up",
    )(*plan, xs, wrow, w_gate, w_up)
    down_spec = pltpu.PrefetchScalarGridSpec(
        num_scalar_prefetch=5,
        grid=(d // tn, n_tiles),
        in_specs=[pl.BlockSpec((tm, ff), row_tile), hbm],
        out_specs=pl.BlockSpec((tm, tn), lambda j, i, *_: (i, j)),
        scratch_shapes=[pltpu.VMEM((2, ff, tn), w_down.dtype), pltpu.SemaphoreType.DMA((2,))],
    )
    return pl.pallas_call(
        functools.partial(_expert_down_kernel, layer=layer, tn=tn),
        grid_spec=down_spec,
        out_shape=jax.ShapeDtypeStruct((n, d), F32),
        compiler_params=_cp(("arbitrary", "arbitrary")),
        name="expert_down",
    )(*plan, act, w_down)


def _dispatch_plan(e_idx, w_sel, n_exp, tm):
    t = e_idx.shape[1]
    n_tiles = (2 * t + n_exp * (tm - 1)) // tm
    e_flat = e_idx.reshape(-1)
    onehot = (e_flat[:, None] == jnp.arange(n_exp, dtype=I32)[None, :]).astype(I32)
    csum = jnp.cumsum(onehot, axis=0)
    rank = jnp.sum((csum - onehot) * onehot, axis=1)
    tiles_e = (csum[-1] + tm - 1) // tm
    tile_end = jnp.cumsum(tiles_e)
    pos = (tile_end - tiles_e)[e_flat] * tm + rank
    n_valid = tile_end[-1]
    pair = jnp.stack([jnp.tile(jnp.arange(t, dtype=I32), 2),
                      lax.bitcast_convert_type(w_sel.reshape(-1), I32)], axis=1)
    rows = jnp.zeros((n_tiles * tm, 2), I32).at[pos].set(pair)
    src = rows[:, 0]
    wrow = lax.bitcast_convert_type(rows[:, 1], F32)
    tile_ids = jnp.arange(n_tiles, dtype=I32)
    valid = tile_ids < n_valid
    te = jnp.minimum(jnp.searchsorted(tile_end, tile_ids, side="right"), n_exp - 1).astype(I32)
    used = tiles_e > 0
    run_of_expert = jnp.cumsum(used.astype(I32)) - 1
    run = jnp.where(valid, run_of_expert[te], 0).astype(I32)
    first = (valid & (tile_ids == (tile_end - tiles_e)[te])).astype(I32)
    run_expert = jnp.nonzero(used, size=n_exp, fill_value=0)[0].astype(I32)
    n_runs = jnp.sum(used.astype(I32)).reshape(1)
    plan = (n_valid.reshape(1).astype(I32), first, run, run_expert, n_runs)
    return src, wrow.reshape(-1, 1), plan, pos.astype(I32)


def _combine_kernel(x_ref, g_ref, c0_ref, c1_ref, o_ref):
    o_ref[...] = x_ref[...] + g_ref[...] * (c0_ref[...] + c1_ref[...])


def _combine(x, gate, contrib, row0, tl):
    bx, lx, d = x.shape
    nl = lx // tl
    off = row0 // tl
    tok = pl.BlockSpec((None, tl, d), lambda b, i: (b, i, 0))
    return pl.pallas_call(
        _combine_kernel,
        grid=(bx, nl),
        in_specs=[tok, _mod_spec(gate, tl),
                  pl.BlockSpec((None, tl, d), lambda b, i: (0, off + b * nl + i, 0)),
                  pl.BlockSpec((None, tl, d), lambda b, i: (1, off + b * nl + i, 0))],
        out_specs=tok,
        out_shape=jax.ShapeDtypeStruct((bx, lx, d), F32),
        compiler_params=_cp(("parallel", "parallel")),
        name="moe_combine",
    )(x, gate, contrib, contrib)


def _moe(f_p, f_s, route_p, route_s, w_gate, w_up, w_down, layer):
    bp, lp, dh = f_p.shape
    d = 2 * dh
    n_p = bp * lp
    n_s = f_s.shape[1]
    n_exp = w_gate.shape[1]
    e_idx = jnp.concatenate([route_p[0], route_s[0]], axis=1)
    w_sel = jnp.concatenate([route_p[1], route_s[1]], axis=1)
    src, wrow, plan, pos = _dispatch_plan(e_idx, w_sel, n_exp, MOE_TM)
    xs = _gather_rows(src, plan[0], [f_p.reshape(n_p, dh), f_s.reshape(n_s, dh)], MOE_TM, jnp.uint32,
                      "moe_dispatch")
    ys = _experts(xs, wrow, plan, w_gate, w_up, w_down, layer, MOE_TM,
                  min(MOE_TF, w_gate.shape[3]), min(MOE_TN, d))
    t = n_p + n_s
    tg = math.gcd(2 * t, LANES)
    contrib = _gather_rows(pos, jnp.full((1,), 2 * t // tg, I32), [ys], tg, F32, "moe_collect")
    return contrib.reshape(2, t, d)


def kernel(x_prompt, x_sample, c_prompt, c_sample, state_ret, state_s5_re, state_s5_im, cache_ckv, cache_kpe, page_table, w_ada, b_ada, norm_mix, norm_ffn, norm_final, w_in_ab, ret_gn, s5_lam_re, s5_lam_im, s5_log_dt, s5_b_re, s5_b_im, s5_c_re, s5_c_im, s5_d, s5_w_glu, s5_b_glu, w_out_ab, w_in_c, q_norm, kv_norm, w_q_up, w_kv_up, w_out_c, w_router, b_router, w_exp_gate, w_exp_up, w_exp_down):
    bp, lp, d = x_prompt.shape
    db = x_sample.shape[0]
    assert x_sample.shape[1] == 1
    depth = w_ada.shape[0]
    past_len = page_table.shape[1] * cache_ckv.shape[2]
    _, _, ret_heads, ret_dk, ret_dv = state_ret.shape
    assert ret_dk == ret_dv and ret_dk % (2 * LANES) == 0
    ret_width = ret_heads * ret_dk
    s5_width = s5_w_glu.shape[1]
    mla_heads = w_q_up.shape[2]
    q_lora = w_q_up.shape[1]
    kv_lora = w_kv_up.shape[1]
    qk_rope = cache_kpe.shape[3]
    qk_nope = w_q_up.shape[3] - qk_rope
    v_head = w_kv_up.shape[3] - qk_nope
    assert qk_nope == LANES and v_head == LANES and qk_rope == LANES // 2
    attn_scale = (qk_nope + qk_rope) ** -0.5

    tl_p = min(256, lp)
    tm_p = min(1024, lp)
    tq = min(512, lp)

    xp = x_prompt
    xs = x_sample.reshape(1, db, d)

    mod = _adaln(jnp.concatenate([c_prompt, c_sample], axis=0), w_ada, b_ada)

    def mods(i):
        mp = [mod[i, :bp, k * d:(k + 1) * d].reshape(bp, 1, d) for k in range(6)]
        ms = [mod[i, bp:, k * d:(k + 1) * d].reshape(1, db, d) for k in range(6)]
        return mp, ms

    pos_p = jnp.arange(lp, dtype=F32)
    pos_s = past_len + jnp.arange(1, dtype=F32)
    w_router_t = w_router.T
    outs = {k: [] for k in ("ret_p", "ret_s", "s5re_p", "s5im_p", "s5re_s", "s5im_s",
                            "ckv_p", "kpe_p", "ckv_s", "kpe_s")}

    for i in range(depth):
        (sh1_p, sc1_p, g1_p, sh2_p, sc2_p, g2_p), (sh1_s, sc1_s, g1_s, sh2_s, sc2_s, g2_s) = mods(i)
        h_p = _norm_mod(xp, norm_mix[i], sh1_p, sc1_p, tl_p)
        h_s = _norm_mod(xs, norm_mix[i], sh1_s, sc1_s, db)
        j = i // 2
        if i % 2 == 0:
            w_in = (w_in_ab, j)
            tn_in = 512
            proj_p = _mm([h_p], w_in, tm=tm_p, tn=tn_in, out_dtype=F32, name="ab_in_p")
            proj_s = _mm([h_s], w_in, tm=db, tn=tn_in, out_dtype=F32, name="ab_in_s")[0]
            chunk = min(RET_CHUNK, lp)
            cos_p, sin_p = _rope_tables(pos_p, ret_dk // 2)
            ret_out_p, ret_p = _retention(proj_p, cos_p, sin_p, _ret_tables(ret_heads, chunk, chunk),
                                          ret_gn[j], None, ret_heads, ret_dk, chunk, math.gcd(ret_heads, 2))
            cos_s, sin_s = _rope_tables(pos_s, ret_dk // 2)
            pad_rows = ((0, RET_PAD - 1), (0, 0))
            proj_s8 = jnp.pad(proj_s[:, None, :4 * ret_width], ((0, 0), (0, RET_PAD - 1), (0, 0)))
            ret_out_s8, ret_s = _retention(proj_s8, jnp.pad(cos_s, pad_rows), jnp.pad(sin_s, pad_rows),
                                           _ret_tables(ret_heads, 1, RET_PAD), ret_gn[j], state_ret[j],
                                           ret_heads, ret_dk, RET_PAD, ret_heads)
            ret_out_s = ret_out_s8[:, 0, :].reshape(1, db, ret_width)
            params = _s5_params(s5_lam_re[j], s5_lam_im[j], s5_log_dt[j], s5_b_re[j], s5_b_im[j],
                                s5_c_re[j], s5_c_im[j])
            col0 = 4 * ret_width // LANES
            z_p, s5re_p, s5im_p = _s5_scan(proj_p, col0, params, s5_d[j], None)
            n_grp, n_state = state_s5_re.shape[2:]
            z_s, s5re_s, s5im_s = _s5_step(proj_s, col0, params, s5_d[j],
                                           state_s5_re[j].reshape(db, -1), state_s5_im[j].reshape(db, -1))
            z_s = z_s.reshape(1, db, s5_width)

            def glu(z, tm):
                tn = min(512, s5_width)
                specs = (pl.BlockSpec((None, tm, tn), lambda b, i_, j_: (b, i_, j_)),
                         pl.BlockSpec((1, tn), lambda b, i_, j_: (0, j_)))
                return _mm([z], (s5_w_glu, j), tm=tm, tn=tn, out_dtype=BF16, epi="glu",
                           extra=(z, s5_b_glu[j].reshape(1, -1)), extra_specs=specs, name="s5_glu")

            s5_out_p = glu(z_p, tm_p)
            s5_out_s = glu(z_s, db)
            xp = _mm_resgate([ret_out_p, s5_out_p], (w_out_ab, j), xp, g1_p, tm_p, 512, "ab_out_p")
            xs = _mm_resgate([ret_out_s, s5_out_s], (w_out_ab, j), xs, g1_s, db, 512, "ab_out_s")
            outs["ret_p"].append(ret_p)
            outs["ret_s"].append(ret_s)
            outs["s5re_p"].append(s5re_p.reshape(bp, n_grp, n_state))
            outs["s5im_p"].append(s5im_p.reshape(bp, n_grp, n_state))
            outs["s5re_s"].append(s5re_s.reshape(db, n_grp, n_state))
            outs["s5im_s"].append(s5im_s.reshape(db, n_grp, n_state))
        else:
            wc = w_in_c[j]
            w_kpe = wc[:, q_lora + kv_lora:]
            half = qk_rope // 2
            w_c = jnp.concatenate([wc, -w_kpe[:, half:], w_kpe[:, :half]], axis=1)
            wq = w_q_up[j]
            wq_pe = wq[:, :, qk_nope:]
            w_q = jnp.concatenate([wq, -wq_pe[:, :, half:], wq_pe[:, :, :half]], axis=2)
            w_q = w_q.reshape(q_lora, mla_heads * 2 * LANES)
            wkv = w_kv_up[j]
            w_k = wkv[:, :, :qk_nope].reshape(kv_lora, mla_heads * qk_nope)
            w_vt = wkv[:, :, qk_nope:].transpose(1, 2, 0)
            w_uk_t = wkv[:, :, :qk_nope].transpose(1, 2, 0)
            w_uv = wkv[:, :, qk_nope:].transpose(1, 0, 2)

            def pe_tables(pos):
                cos, sin = _rope_tables(pos, half)
                z = jnp.zeros((pos.shape[0], LANES - qk_rope), F32)
                return (jnp.concatenate([cos, cos, z], axis=1), jnp.concatenate([sin, sin, z], axis=1))

            def project(h, pos, tm):
                cos_t, sin_t = pe_tables(pos)
                q_lat, ckv, kpe = _mla_in(h, w_c, q_norm[j], kv_norm[j], cos_t, sin_t, q_lora, kv_lora,
                                          tm, min(512, d))
                if pos.shape[0] == 1:
                    tab = pl.BlockSpec((1, LANES), lambda b, i_, j_: (0, 0))
                else:
                    tab = pl.BlockSpec((tm, LANES), lambda b, i_, j_: (i_, 0))
                q = _mm([q_lat], w_q, tm=tm, tn=2 * LANES, out_dtype=BF16, epi="rope", out_scale=attn_scale,
                        extra=(cos_t, sin_t), extra_specs=(tab, tab), name="mla_q_up")
                return q, ckv, kpe

            q_p, ckv_p, kpe_p = project(h_p, pos_p, tm_p)
            kspec = (pl.BlockSpec((None, tm_p, LANES), lambda b, i_, j_: (b, i_, 0)),)
            k_p = _mm([ckv_p], w_k, tm=tm_p, tn=LANES, tn_out=2 * LANES, out_dtype=BF16, epi="kcat",
                      extra=(kpe_p,), extra_specs=kspec, name="mla_k_up")
            vt_p = _vt_up(ckv_p, w_vt, tm_p)
            att_p = _flash(q_p, k_p, vt_p, mla_heads, 2 * LANES, v_head, tq)
            xp = _mm_resgate([att_p], (w_out_c, j), xp, g1_p, tm_p, 512, "c_out_p")
            q_s, ckv_s, kpe_s = project(h_s, pos_s, db)
            q_s2 = q_s.reshape(db, mla_heads * 2 * LANES)
            q_abs = _bmm_heads(q_s2, w_uk_t, lambda h_: 2 * h_, BF16, "mla_q_absorb")
            o_lat = _decode(page_table, q_abs.reshape(db, mla_heads, kv_lora),
                            q_s2.reshape(db, mla_heads, 2 * LANES), ckv_s.reshape(db, 1, kv_lora),
                            kpe_s.reshape(db, 1, LANES), cache_ckv, jnp.swapaxes(cache_kpe, 2, 3), j, mla_heads,
                            qk_rope)
            att_s = _bmm_heads(o_lat.reshape(db, mla_heads * kv_lora), w_uv, lambda h_: h_, BF16, "mla_v_absorb")
            xs = _mm_resgate([att_s.reshape(1, db, mla_heads * v_head)], (w_out_c, j), xs, g1_s, db, 512,
                             "c_out_s")
            outs["ckv_p"].append(ckv_p)
            outs["kpe_p"].append(kpe_p[:, :, :qk_rope])
            outs["ckv_s"].append(ckv_s.reshape(db, 1, kv_lora))
            outs["kpe_s"].append(kpe_s.reshape(db, 1, LANES)[:, :, :qk_rope])

        f_p, e_p, wsel_p = _ffn_norm_route(xp, norm_ffn[i], sh2_p, sc2_p, w_router_t, b_router, tl_p)
        f_s, e_s, wsel_s = _ffn_norm_route(xs, norm_ffn[i], sh2_s, sc2_s, w_router_t, b_router, db)
        contrib = _moe(f_p, f_s, (e_p, wsel_p), (e_s, wsel_s), w_exp_gate, w_exp_up, w_exp_down, i)
        xp = _combine(xp, g2_p, contrib, 0, tl_p)
        xs = _combine(xs, g2_s, contrib, bp * lp, db)

    y_prompt = _rmsnorm(xp, norm_final, tl_p)
    y_sample = _rmsnorm(xs, norm_final, db).reshape(db, 1, d)
    st = lambda k: jnp.stack(outs[k])
    return (y_prompt, y_sample, st("ret_p"), st("ret_s"), st("s5re_p"), st("s5im_p"), st("s5re_s"),
            st("s5im_s"), st("ckv_p"), st("kpe_p"), st("ckv_s"), st("kpe_s"))
```

```python
import functools
import math

import jax
import jax.numpy as jnp
from jax import lax
from jax.experimental import pallas as pl
from jax.experimental.pallas import tpu as pltpu

F32 = jnp.float32
BF16 = jnp.bfloat16
I32 = jnp.int32

NORM_EPS = 1e-6
GN_EPS = 1e-5
NEG_INF = -1e30
ROPE_THETA = 10000.0
RET_CHUNK = 128
N_EXPERT_GROUPS = 4
LANES = 128
SUBLANES = 8
VMEM_LIMIT = 56 * 1024 * 1024
MOE_TM = 256
MOE_TF = 512
MOE_TN = 2048
GATHER_UNROLL = 8
RET_PAD = 16
DECODE_PAGES = 32


def _cp(sem):
    return pltpu.CompilerParams(dimension_semantics=sem, vmem_limit_bytes=VMEM_LIMIT)


def _bdot(a, b):
    return jnp.dot(a.astype(BF16), b.astype(BF16), preferred_element_type=F32)


def _bdot_nt(a, b):
    return lax.dot_general(a.astype(BF16), b.astype(BF16), (((1,), (1,)), ((), ())),
                           preferred_element_type=F32)


def _bdot_tn(a, b):
    return lax.dot_general(a.astype(BF16), b.astype(BF16), (((0,), (0,)), ((), ())),
                           preferred_element_type=F32)


def _adaln_kernel(c_ref, w_ref, b_ref, o_ref):
    o_ref[...] = _bdot(jax.nn.silu(c_ref[...]), w_ref[...]) + b_ref[...]


def _adaln(c_all, w_ada, b_ada, tn=512):
    depth, d, n = w_ada.shape
    bc = c_all.shape[0]
    return pl.pallas_call(
        _adaln_kernel,
        grid=(depth, n // tn),
        in_specs=[pl.BlockSpec((bc, d), lambda l, j: (0, 0)),
                  pl.BlockSpec((None, d, tn), lambda l, j: (l, 0, j)),
                  pl.BlockSpec((None, 1, tn), lambda l, j: (l, 0, j))],
        out_specs=pl.BlockSpec((None, bc, tn), lambda l, j: (l, 0, j)),
        out_shape=jax.ShapeDtypeStruct((depth, bc, n), F32),
        compiler_params=_cp(("parallel", "parallel")),
        name="adaln",
    )(c_all, w_ada, b_ada.reshape(depth, 1, n))


def _rms(x, g):
    return x * lax.rsqrt(jnp.mean(x * x, axis=-1, keepdims=True) + NORM_EPS) * g


def _mod_spec(m, tl):
    if m.shape[1] == 1:
        return pl.BlockSpec((None, 1, m.shape[2]), lambda b, i: (b, 0, 0))
    return pl.BlockSpec((None, tl, m.shape[2]), lambda b, i: (b, i, 0))


def _norm_mod_kernel(x_ref, g_ref, sh_ref, sc_ref, o_ref):
    y = _rms(x_ref[...], g_ref[...])
    o_ref[...] = (y * (1.0 + sc_ref[...]) + sh_ref[...]).astype(o_ref.dtype)


def _norm_mod(x, g, shift, scale, tl):
    bx, lx, d = x.shape
    return pl.pallas_call(
        _norm_mod_kernel,
        grid=(bx, lx // tl),
        in_specs=[pl.BlockSpec((None, tl, d), lambda b, i: (b, i, 0)),
                  pl.BlockSpec((1, d), lambda b, i: (0, 0)),
                  _mod_spec(shift, tl), _mod_spec(scale, tl)],
        out_specs=pl.BlockSpec((None, tl, d), lambda b, i: (b, i, 0)),
        out_shape=jax.ShapeDtypeStruct((bx, lx, d), BF16),
        compiler_params=_cp(("parallel", "parallel")),
        name="norm_mod",
    )(x, g.reshape(1, d), shift, scale)


def _rmsnorm_kernel(x_ref, g_ref, o_ref):
    o_ref[...] = _rms(x_ref[...], g_ref[...])


def _rmsnorm(x, g, tl):
    bx, lx, d = x.shape
    return pl.pallas_call(
        _rmsnorm_kernel,
        grid=(bx, lx // tl),
        in_specs=[pl.BlockSpec((None, tl, d), lambda b, i: (b, i, 0)),
                  pl.BlockSpec((1, d), lambda b, i: (0, 0))],
        out_specs=pl.BlockSpec((None, tl, d), lambda b, i: (b, i, 0)),
        out_shape=jax.ShapeDtypeStruct((bx, lx, d), F32),
        compiler_params=_cp(("parallel", "parallel")),
        name="rmsnorm",
    )(x, g.reshape(1, d))


def _first_argmax(v, iota, size):
    m = jnp.max(v, axis=1, keepdims=True)
    pos = jnp.min(jnp.where(v == m, iota, size), axis=1, keepdims=True)
    return m, pos


def _ffn_norm_route_kernel(x_ref, g_ref, sh_ref, sc_ref, wr_ref, br_ref, f_ref, e_ref, w_ref, *, n_groups):
    y = _rms(x_ref[...], g_ref[...])
    f = y * (1.0 + sc_ref[...]) + sh_ref[...]
    half = f.shape[1] // 2
    bits = lax.bitcast_convert_type(f.astype(BF16).astype(F32), jnp.uint32)
    f_ref[...] = bits[:, half:] | (bits[:, :half] >> 16)
    logits = lax.dot_general(wr_ref[...], f, (((1,), (1,)), ((), ())),
                             precision=lax.Precision.HIGHEST,
                             preferred_element_type=F32)
    n_exp, tl = logits.shape
    epg = n_exp // n_groups
    scores = jax.nn.sigmoid(logits)
    sel = (scores + br_ref[...]).reshape(n_groups, epg, tl)
    sc3 = scores.reshape(n_groups, epg, tl)
    iota_e = lax.broadcasted_iota(I32, (n_groups, epg, tl), 1)
    m1, p1 = _first_argmax(sel, iota_e, epg)
    m2, p2 = _first_argmax(jnp.where(iota_e == p1, -jnp.inf, sel), iota_e, epg)
    gscore = m1 + m2
    iota_g = lax.broadcasted_iota(I32, (n_groups, 1, tl), 0)
    gmax = jnp.max(gscore, axis=0, keepdims=True)
    g_idx = jnp.min(jnp.where(gscore == gmax, iota_g, n_groups), axis=0, keepdims=True)
    sc_g, p1_g, p2_g = sc3[0:1], p1[0:1], p2[0:1]
    for gi in range(1, n_groups):
        pick = g_idx == gi
        sc_g = jnp.where(pick, sc3[gi:gi + 1], sc_g)
        p1_g = jnp.where(pick, p1[gi:gi + 1], p1_g)
        p2_g = jnp.where(pick, p2[gi:gi + 1], p2_g)
    iota_l = lax.broadcasted_iota(I32, (1, epg, tl), 1)
    s1 = jnp.sum(jnp.where(iota_l == p1_g, sc_g, 0.0), axis=1, keepdims=True)
    s2 = jnp.sum(jnp.where(iota_l == p2_g, sc_g, 0.0), axis=1, keepdims=True)
    tot = s1 + s2
    e_ref[0:1, :] = (g_idx * epg + p1_g).reshape(1, tl)
    e_ref[1:2, :] = (g_idx * epg + p2_g).reshape(1, tl)
    w_ref[0:1, :] = (s1 / tot).reshape(1, tl)
    w_ref[1:2, :] = (s2 / tot).reshape(1, tl)


def _ffn_norm_route(x, g, shift, scale, w_router_t, b_router, tl):
    bx, lx, d = x.shape
    n_exp = w_router_t.shape[0]
    nl = lx // tl
    return pl.pallas_call(
        functools.partial(_ffn_norm_route_kernel, n_groups=N_EXPERT_GROUPS),
        grid=(bx, nl),
        in_specs=[pl.BlockSpec((None, tl, d), lambda b, i: (b, i, 0)),
                  pl.BlockSpec((1, d), lambda b, i: (0, 0)),
                  _mod_spec(shift, tl), _mod_spec(scale, tl),
                  pl.BlockSpec((n_exp, d), lambda b, i: (0, 0)),
                  pl.BlockSpec((n_exp, 1), lambda b, i: (0, 0))],
        out_specs=[pl.BlockSpec((None, tl, d // 2), lambda b, i: (b, i, 0)),
                   pl.BlockSpec((2, tl), lambda b, i: (0, b * nl + i)),
                   pl.BlockSpec((2, tl), lambda b, i: (0, b * nl + i))],
        out_shape=[jax.ShapeDtypeStruct((bx, lx, d // 2), jnp.uint32),
                   jax.ShapeDtypeStruct((2, bx * lx), I32),
                   jax.ShapeDtypeStruct((2, bx * lx), F32)],
        compiler_params=_cp(("parallel", "parallel")),
        name="ffn_norm_route",
    )(x, g.reshape(1, d), shift, scale, w_router_t, b_router.reshape(n_exp, 1))


def _mm_kernel(*refs, n_a, epi, out_scale):
    a_refs = refs[:n_a]
    w_ref = refs[n_a]
    extra = refs[n_a + 1:-1]
    o_ref = refs[-1]
    acc = None
    k0 = 0
    for a_ref in a_refs:
        kk = a_ref.shape[-1]
        part = _bdot(a_ref[...], w_ref[k0:k0 + kk, :])
        acc = part if acc is None else acc + part
        k0 += kk
    if epi == "none":
        out = acc
    elif epi == "resgate":
        res_ref, gate_ref = extra
        out = res_ref[...] + gate_ref[...] * acc
    elif epi == "glu":
        z_ref, b_ref = extra
        out = z_ref[...] * jax.nn.sigmoid(acc + b_ref[...])
    elif epi == "rope":
        cos_ref, sin_ref = extra
        parts = []
        for h in range(acc.shape[1] // (2 * LANES)):
            parts.append(acc[:, 2 * h * LANES:(2 * h + 1) * LANES])
            t = acc[:, (2 * h + 1) * LANES:(2 * h + 2) * LANES]
            parts.append(t * cos_ref[...] + pltpu.roll(t, LANES // 2, 1) * sin_ref[...])
        out = jnp.concatenate(parts, axis=1) * out_scale
    elif epi == "kcat":
        kpe_ref, = extra
        out = jnp.concatenate([acc, kpe_ref[...]], axis=1)
    o_ref[...] = out.astype(o_ref.dtype)


def _mm(a_list, w, *, tm, tn, out_dtype, epi="none", extra=(), extra_specs=(), tn_out=None, out_scale=1.0,
        name="mm"):
    bx, lx, _ = a_list[0].shape
    tn_out = tn if tn_out is None else tn_out
    in_specs = [pl.BlockSpec((None, tm, a.shape[2]), lambda b, i, j: (b, i, 0)) for a in a_list]
    if isinstance(w, tuple):
        w, layer = w
        _, k, n = w.shape
        in_specs.append(pl.BlockSpec((None, k, tn), lambda b, i, j: (layer, 0, j)))
    else:
        k, n = w.shape
        in_specs.append(pl.BlockSpec((k, tn), lambda b, i, j: (0, j)))
    n_out = n // tn * tn_out
    in_specs.extend(extra_specs)
    return pl.pallas_call(
        functools.partial(_mm_kernel, n_a=len(a_list), epi=epi, out_scale=out_scale),
        grid=(bx, lx // tm, n // tn),
        in_specs=in_specs,
        out_specs=pl.BlockSpec((None, tm, tn_out), lambda b, i, j: (b, i, j)),
        out_shape=jax.ShapeDtypeStruct((bx, lx, n_out), out_dtype),
        compiler_params=_cp(("parallel", "parallel", "arbitrary")),
        name=name,
    )(*a_list, w, *extra)


def _gate_spec(gate, tm, tn):
    if gate.shape[1] == 1:
        return pl.BlockSpec((None, 1, tn), lambda b, i, j: (b, 0, j))
    return pl.BlockSpec((None, tm, tn), lambda b, i, j: (b, i, j))


def _mm_resgate(a_list, w, res, gate, tm, tn, name):
    specs = (pl.BlockSpec((None, tm, tn), lambda b, i, j: (b, i, j)), _gate_spec(gate, tm, tn))
    return _mm(a_list, w, tm=tm, tn=tn, out_dtype=F32, epi="resgate", extra=(res, gate),
               extra_specs=specs, name=name)


def _bmm_kernel(a_ref, w_ref, o_ref):
    o_ref[...] = _bdot(a_ref[...], w_ref[...]).astype(o_ref.dtype)


def _bmm_heads(a, w, a_block_of_head, out_dtype, name):
    m = a.shape[0]
    h, ka, n = w.shape
    return pl.pallas_call(
        _bmm_kernel,
        grid=(h,),
        in_specs=[pl.BlockSpec((m, ka), lambda i: (0, a_block_of_head(i))),
                  pl.BlockSpec((None, ka, n), lambda i: (i, 0, 0))],
        out_specs=pl.BlockSpec((m, n), lambda i: (0, i)),
        out_shape=jax.ShapeDtypeStruct((m, h * n), out_dtype),
        compiler_params=_cp(("parallel",)),
        name=name,
    )(a, w)


def _rope_halves(x, cos, sin):
    half = x.shape[-1] // 2
    x1, x2 = x[:, :half], x[:, half:]
    return jnp.concatenate([x1 * cos - x2 * sin, x1 * sin + x2 * cos], axis=-1)


def _retention_kernel(*refs, has_s0, k_scale):
    if has_s0:
        (q_ref, k_ref, v_ref, g_ref, cos_ref, sin_ref, intra_ref, qdec_ref, kdec_ref, cdec_ref, gn_ref,
         s0_ref, o_ref, sfin_ref, s_scr) = refs
    else:
        (q_ref, k_ref, v_ref, g_ref, cos_ref, sin_ref, intra_ref, qdec_ref, kdec_ref, cdec_ref, gn_ref,
         o_ref, sfin_ref, s_scr) = refs
    c = pl.program_id(2)

    @pl.when(c == 0)
    def _():
        if has_s0:
            s_scr[...] = s0_ref[...]
        else:
            s_scr[...] = jnp.zeros_like(s_scr)

    cos, sin = cos_ref[...], sin_ref[...]
    hb, dk, _ = s_scr.shape
    last = c == pl.num_programs(2) - 1
    for hh in range(hb):
        cols = slice(hh * dk, (hh + 1) * dk)
        q = _rope_halves(q_ref[:, cols], cos, sin)
        k = _rope_halves(k_ref[:, cols], cos, sin) * k_scale
        v = v_ref[:, cols]
        s = s_scr[hh]
        att = _bdot_nt(q, k) * intra_ref[hh]
        o = _bdot(att, v) + _bdot(q, s) * qdec_ref[hh]
        s_new = s * cdec_ref[hh] + _bdot_tn(k * kdec_ref[hh], v)
        s_scr[hh] = s_new

        @pl.when(last)
        def _():
            sfin_ref[hh] = s_new

        mu = jnp.mean(o, axis=-1, keepdims=True)
        var = jnp.mean(jnp.square(o - mu), axis=-1, keepdims=True)
        on = (o - mu) * lax.rsqrt(var + GN_EPS) * gn_ref[:, cols]
        o_ref[:, cols] = (jax.nn.silu(g_ref[:, cols]) * on).astype(o_ref.dtype)


def _ret_tables(n_heads, chunk, pad_to):
    lg = jnp.log1p(-jnp.exp2(-5.0 - jnp.arange(n_heads, dtype=F32)))
    idx = jnp.arange(chunk, dtype=F32)
    diff = idx[:, None] - idx[None, :]
    intra = jnp.where(diff[None] >= 0, jnp.exp(jnp.maximum(diff, 0.0)[None] * lg[:, None, None]), 0.0)
    q_dec = jnp.exp((idx + 1.0)[None, :] * lg[:, None])[:, :, None]
    k_dec = jnp.exp((chunk - 1.0 - idx)[None, :] * lg[:, None])[:, :, None]
    c_dec = jnp.exp(chunk * lg)[:, None, None]
    p = pad_to - chunk
    if p:
        intra = jnp.pad(intra, ((0, 0), (0, p), (0, p)))
        q_dec = jnp.pad(q_dec, ((0, 0), (0, p), (0, 0)))
        k_dec = jnp.pad(k_dec, ((0, 0), (0, p), (0, 0)))
    return intra, q_dec, k_dec, c_dec


def _rope_tables(pos, half):
    inv = jnp.exp(-math.log(ROPE_THETA) * jnp.arange(half, dtype=F32) / half)
    ang = pos[:, None] * inv[None, :]
    return jnp.cos(ang), jnp.sin(ang)


def _retention(proj, cos, sin, tables, gn, s0, n_heads, dk, chunk, hb):
    b, l, _ = proj.shape
    nc = l // chunk
    nh = n_heads // hb
    intra, q_dec, k_dec, c_dec = tables
    has_s0 = s0 is not None

    def col(off):
        return pl.BlockSpec((None, chunk, hb * dk), lambda bi, h, c: (bi, c, off + h))

    in_specs = [col(0), col(nh), col(2 * nh), col(3 * nh),
                pl.BlockSpec((chunk, dk // 2), lambda bi, h, c: (c, 0)),
                pl.BlockSpec((chunk, dk // 2), lambda bi, h, c: (c, 0)),
                pl.BlockSpec((hb, chunk, chunk), lambda bi, h, c: (h, 0, 0)),
                pl.BlockSpec((hb, chunk, 1), lambda bi, h, c: (h, 0, 0)),
                pl.BlockSpec((hb, chunk, 1), lambda bi, h, c: (h, 0, 0)),
                pl.BlockSpec((hb, 1, 1), lambda bi, h, c: (h, 0, 0)),
                pl.BlockSpec((1, hb * dk), lambda bi, h, c: (0, h))]
    args = [proj, proj, proj, proj, cos, sin, intra, q_dec, k_dec, c_dec, gn.reshape(1, -1)]
    if has_s0:
        in_specs.append(pl.BlockSpec((None, hb, dk, dk), lambda bi, h, c: (bi, h, 0, 0)))
        args.append(s0)
    return pl.pallas_call(
        functools.partial(_retention_kernel, has_s0=has_s0, k_scale=dk ** -0.5),
        grid=(b, nh, nc),
        in_specs=in_specs,
        out_specs=[pl.BlockSpec((None, chunk, hb * dk), lambda bi, h, c: (bi, c, h)),
                   pl.BlockSpec((None, hb, dk, dk), lambda bi, h, c: (bi, h, 0, 0))],
        out_shape=[jax.ShapeDtypeStruct((b, l, n_heads * dk), BF16),
                   jax.ShapeDtypeStruct((b, n_heads, dk, dk), F32)],
        scratch_shapes=[pltpu.VMEM((hb, dk, dk), F32)],
        compiler_params=_cp(("parallel", "parallel", "arbitrary")),
        name="retention",
    )(*args)


def _cmul(ar, ai, br, bi):
    return ar * br - ai * bi, ar * bi + ai * br


def _s5_params(lam_re, lam_im, log_dt, b_re, b_im, c_re, c_im):
    g, p, gs = b_re.shape
    gpb = LANES // gs
    nj = g // gpb
    lam = lax.complex(lam_re.astype(F32), lam_im.astype(F32))
    dt = jnp.exp(log_dt.astype(F32))[:, None]
    a_bar = jnp.exp(lam * dt)
    b_bar = ((a_bar - 1.0) / lam)[:, :, None] * lax.complex(b_re.astype(F32), b_im.astype(F32))
    eye = jnp.eye(gpb, dtype=F32)

    def bdiag_in(m):
        m = m.reshape(nj, gpb, p, gs).transpose(0, 1, 3, 2)
        return (m[:, :, :, None, :] * eye[None, :, None, :, None]).reshape(nj, gpb * gs, gpb * p)

    def bdiag_out(m):
        m = m.reshape(nj, gpb, gs, p).transpose(0, 1, 3, 2)
        return (m[:, :, :, None, :] * eye[None, :, None, :, None]).reshape(nj, gpb * p, gpb * gs)

    b_blk = jnp.concatenate([bdiag_in(jnp.real(b_bar)), bdiag_in(jnp.imag(b_bar))], axis=2)
    c_blk = jnp.concatenate([bdiag_out(c_re.astype(F32)), bdiag_out(-c_im.astype(F32))], axis=1)
    a_re = jnp.real(a_bar).reshape(nj, 1, gpb * p)
    a_im = jnp.imag(a_bar).reshape(nj, 1, gpb * p)
    return b_blk, c_blk, a_re, a_im


def _s5_scan_kernel(*refs, has_h0, n_seg):
    if has_h0:
        (u_ref, b_ref, c_ref, ar_ref, ai_ref, d_ref, h0r_ref, h0i_ref, z_ref, hr_ref, hi_ref,
         up_scr, hs_scr, yp_scr) = refs
    else:
        (u_ref, b_ref, c_ref, ar_ref, ai_ref, d_ref, z_ref, hr_ref, hi_ref, up_scr, hs_scr, yp_scr) = refs
    l = u_ref.shape[0]
    ns = ar_ref.shape[1]
    seg = l // n_seg
    for s in range(n_seg):
        up_scr[pl.ds(s, seg, stride=n_seg), :] = u_ref[pl.ds(s * seg, seg), :]
    up = up_scr[...]
    hs_scr[...] = _bdot(up, b_ref[...])
    ar = jnp.broadcast_to(ar_ref[...], (n_seg, ns))
    ai = jnp.broadcast_to(ai_ref[...], (n_seg, ns))

    def rows(t):
        return hs_scr[pl.ds(pl.multiple_of(t * n_seg, n_seg), n_seg), :]

    def local_step(t, carry):
        hr, hi = carry
        pr, pi = _cmul(ar, ai, hr, hi)
        blk = rows(t)
        return pr + blk[:, :ns], pi + blk[:, ns:]

    zeros = jnp.zeros((n_seg, ns), F32)
    er, ei = lax.fori_loop(0, seg, local_step, (zeros, zeros))

    pr, pi = None, None
    br_, bi_ = ar_ref[...], ai_ref[...]
    e = seg
    while e:
        if e & 1:
            pr, pi = (br_, bi_) if pr is None else _cmul(pr, pi, br_, bi_)
        e >>= 1
        if e:
            br_, bi_ = _cmul(br_, bi_, br_, bi_)

    if has_h0:
        cr, ci = h0r_ref[...], h0i_ref[...]
    else:
        cr, ci = jnp.zeros((1, ns), F32), jnp.zeros((1, ns), F32)
    row = lax.broadcasted_iota(I32, (n_seg, ns), 0)
    ir, ii = jnp.zeros((n_seg, ns), F32), jnp.zeros((n_seg, ns), F32)
    for s in range(n_seg):
        ir = jnp.where(row == s, cr, ir)
        ii = jnp.where(row == s, ci, ii)
        qr, qi = _cmul(pr, pi, cr, ci)
        cr, ci = qr + er[s:s + 1], qi + ei[s:s + 1]
    hr_ref[...] = cr
    hi_ref[...] = ci

    def full_step(t, carry):
        hr, hi = carry
        qr_, qi_ = _cmul(ar, ai, hr, hi)
        blk = rows(t)
        nr = qr_ + blk[:, :ns]
        ni = qi_ + blk[:, ns:]
        hs_scr[pl.ds(pl.multiple_of(t * n_seg, n_seg), n_seg), :] = jnp.concatenate([nr, ni], axis=1)
        return nr, ni

    lax.fori_loop(0, seg, full_step, (ir, ii))
    yp_scr[...] = jax.nn.gelu(_bdot(hs_scr[...], c_ref[...]) + d_ref[...] * up)
    for s in range(n_seg):
        z_ref[pl.ds(s * seg, seg), :] = yp_scr[pl.ds(s, seg, stride=n_seg), :]


def _s5_scan(proj, col0, params, d, h0, n_seg=SUBLANES):
    b_blk, c_blk, a_re, a_im = params
    b, l, _ = proj.shape
    nj, _, ns2 = b_blk.shape
    ns = ns2 // 2
    has_h0 = h0 is not None
    in_specs = [pl.BlockSpec((None, l, LANES), lambda bi, j: (bi, 0, col0 + j)),
                pl.BlockSpec((None, LANES, ns2), lambda bi, j: (j, 0, 0)),
                pl.BlockSpec((None, ns2, LANES), lambda bi, j: (j, 0, 0)),
                pl.BlockSpec((None, 1, ns), lambda bi, j: (j, 0, 0)),
                pl.BlockSpec((None, 1, ns), lambda bi, j: (j, 0, 0)),
                pl.BlockSpec((1, LANES), lambda bi, j: (0, j))]
    args = [proj, b_blk, c_blk, a_re, a_im, d.reshape(1, -1)]
    st_spec = pl.BlockSpec((None, 1, ns), lambda bi, j: (bi, 0, j))
    if has_h0:
        in_specs += [st_spec, st_spec]
        args += [h0[0].reshape(b, 1, -1), h0[1].reshape(b, 1, -1)]
    return pl.pallas_call(
        functools.partial(_s5_scan_kernel, has_h0=has_h0, n_seg=n_seg),
        grid=(b, nj),
        in_specs=in_specs,
        out_specs=[pl.BlockSpec((None, l, LANES), lambda bi, j: (bi, 0, j)), st_spec, st_spec],
        out_shape=[jax.ShapeDtypeStruct((b, l, nj * LANES), F32),
                   jax.ShapeDtypeStruct((b, 1, nj * ns), F32),
                   jax.ShapeDtypeStruct((b, 1, nj * ns), F32)],
        scratch_shapes=[pltpu.VMEM((l, LANES), F32), pltpu.VMEM((l, ns2), F32), pltpu.VMEM((l, LANES), F32)],
        compiler_params=_cp(("parallel", "parallel")),
        name="s5_scan",
    )(*args)


def _s5_step_kernel(u_ref, b_ref, c_ref, ar_ref, ai_ref, d_ref, h0r_ref, h0i_ref, z_ref, hr_ref, hi_ref):
    ns = ar_ref.shape[1]
    u = u_ref[...]
    bu = _bdot(u, b_ref[...])
    pr, pi = _cmul(ar_ref[...], ai_ref[...], h0r_ref[...], h0i_ref[...])
    hr = pr + bu[:, :ns]
    hi = pi + bu[:, ns:]
    hr_ref[...] = hr
    hi_ref[...] = hi
    y = _bdot(jnp.concatenate([hr, hi], axis=1), c_ref[...]) + d_ref[...] * u
    z_ref[...] = jax.nn.gelu(y)


def _s5_step(proj, col0, params, d, h0r, h0i):
    b_blk, c_blk, a_re, a_im = params
    b = proj.shape[0]
    nj, _, ns2 = b_blk.shape
    ns = ns2 // 2
    st_spec = pl.BlockSpec((b, ns), lambda j: (0, j))
    return pl.pallas_call(
        _s5_step_kernel,
        grid=(nj,),
        in_specs=[pl.BlockSpec((b, LANES), lambda j: (0, col0 + j)),
                  pl.BlockSpec((None, LANES, ns2), lambda j: (j, 0, 0)),
                  pl.BlockSpec((None, ns2, LANES), lambda j: (j, 0, 0)),
                  pl.BlockSpec((None, 1, ns), lambda j: (j, 0, 0)),
                  pl.BlockSpec((None, 1, ns), lambda j: (j, 0, 0)),
                  pl.BlockSpec((1, LANES), lambda j: (0, j)),
                  st_spec, st_spec],
        out_specs=[pl.BlockSpec((b, LANES), lambda j: (0, j)), st_spec, st_spec],
        out_shape=[jax.ShapeDtypeStruct((b, nj * LANES), F32),
                   jax.ShapeDtypeStruct((b, nj * ns), F32),
                   jax.ShapeDtypeStruct((b, nj * ns), F32)],
        compiler_params=_cp(("parallel",)),
        name="s5_step",
    )(proj, b_blk, c_blk, a_re, a_im, d.reshape(1, -1), h0r, h0i)


def _mla_in_kernel(a_ref, w_ref, qn_ref, kvn_ref, cos_ref, sin_ref, ql_ref, ckv_ref, kpe_ref, acc_scr,
                   *, q_lora, kv_lora):
    kk = pl.program_id(2)

    @pl.when(kk == 0)
    def _():
        acc_scr[...] = jnp.zeros_like(acc_scr)

    acc_scr[...] += _bdot(a_ref[...], w_ref[...])

    @pl.when(kk == pl.num_programs(2) - 1)
    def _():
        acc = acc_scr[...]
        ql_ref[...] = _rms(acc[:, :q_lora], qn_ref[...]).astype(ql_ref.dtype)
        ckv_ref[...] = _rms(acc[:, q_lora:q_lora + kv_lora], kvn_ref[...])
        t = acc[:, q_lora + kv_lora:]
        kpe_ref[...] = t * cos_ref[...] + pltpu.roll(t, LANES // 2, 1) * sin_ref[...]


def _mla_in(h, w_c, q_norm, kv_norm, cos_t, sin_t, q_lora, kv_lora, tm, tk):
    bx, lx, d = h.shape
    n = w_c.shape[1]
    if cos_t.shape[0] == 1:
        tab = pl.BlockSpec((1, LANES), lambda b, i, k: (0, 0))
    else:
        tab = pl.BlockSpec((tm, LANES), lambda b, i, k: (i, 0))
    return pl.pallas_call(
        functools.partial(_mla_in_kernel, q_lora=q_lora, kv_lora=kv_lora),
        grid=(bx, lx // tm, d // tk),
        in_specs=[pl.BlockSpec((None, tm, tk), lambda b, i, k: (b, i, k)),
                  pl.BlockSpec((tk, n), lambda b, i, k: (k, 0)),
                  pl.BlockSpec((1, q_lora), lambda b, i, k: (0, 0)),
                  pl.BlockSpec((1, kv_lora), lambda b, i, k: (0, 0)),
                  tab, tab],
        out_specs=[pl.BlockSpec((None, tm, q_lora), lambda b, i, k: (b, i, 0)),
                   pl.BlockSpec((None, tm, kv_lora), lambda b, i, k: (b, i, 0)),
                   pl.BlockSpec((None, tm, LANES), lambda b, i, k: (b, i, 0))],
        out_shape=[jax.ShapeDtypeStruct((bx, lx, q_lora), BF16),
                   jax.ShapeDtypeStruct((bx, lx, kv_lora), F32),
                   jax.ShapeDtypeStruct((bx, lx, LANES), F32)],
        scratch_shapes=[pltpu.VMEM((tm, n), F32)],
        compiler_params=_cp(("parallel", "parallel", "arbitrary")),
        name="mla_in",
    )(h, w_c, q_norm.reshape(1, -1), kv_norm.reshape(1, -1), cos_t, sin_t)


def _vt_up_kernel(w_ref, c_ref, o_ref):
    o_ref[...] = _bdot_nt(w_ref[...], c_ref[...]).astype(o_ref.dtype)


def _vt_up(ckv, w_vt, tl):
    b, l, r = ckv.shape
    h, dv, _ = w_vt.shape
    return pl.pallas_call(
        _vt_up_kernel,
        grid=(b, l // tl, h),
        in_specs=[pl.BlockSpec((None, dv, r), lambda bi, i, hh: (hh, 0, 0)),
                  pl.BlockSpec((None, tl, r), lambda bi, i, hh: (bi, i, 0))],
        out_specs=pl.BlockSpec((None, dv, tl), lambda bi, i, hh: (bi, hh, i)),
        out_shape=jax.ShapeDtypeStruct((b, h * dv, l), BF16),
        compiler_params=_cp(("parallel", "parallel", "arbitrary")),
        name="mla_vt_up",
    )(w_vt, ckv)


def _flash_kernel(q_ref, k_ref, v_ref, o_ref, m_scr, l_scr, acc_scr, *, tq, hp, dq, dv):
    i = pl.program_id(2)
    m_scr[...] = jnp.full_like(m_scr, NEG_INF)
    l_scr[...] = jnp.zeros_like(l_scr)
    acc_scr[...] = jnp.zeros_like(acc_scr)

    def block(j, masked):
        start = pl.multiple_of(j * tq, tq)
        for h in range(hp):
            q = q_ref[:, h * dq:(h + 1) * dq]
            k = k_ref[pl.ds(start, tq), h * dq:(h + 1) * dq]
            vt = v_ref[h * dv:(h + 1) * dv, pl.ds(start, tq)]
            st = _bdot_nt(k, q)
            if masked:
                kr = lax.broadcasted_iota(I32, (tq, tq), 0)
                qc = lax.broadcasted_iota(I32, (tq, tq), 1)
                st = jnp.where(kr <= qc, st, NEG_INF)
            m_old = m_scr[h]
            m_new = jnp.maximum(m_old, jnp.max(st, axis=0, keepdims=True))
            alpha = jnp.exp(m_old - m_new)
            p = jnp.exp(st - m_new)
            l_scr[h] = alpha * l_scr[h] + jnp.sum(p, axis=0, keepdims=True)
            acc_scr[h] = alpha * acc_scr[h] + _bdot(vt, p)
            m_scr[h] = m_new

    def body(j, carry):
        block(j, False)
        return carry

    lax.fori_loop(0, i, body, 0)
    block(i, True)
    for h in range(hp):
        o_ref[:, h * dv:(h + 1) * dv] = (acc_scr[h] / l_scr[h]).T.astype(o_ref.dtype)


def _flash(q, k, vt, n_heads, dq, dv, tq):
    b, l, _ = q.shape
    hp = 2 if n_heads % 2 == 0 else 1
    return pl.pallas_call(
        functools.partial(_flash_kernel, tq=tq, hp=hp, dq=dq, dv=dv),
        grid=(b, n_heads // hp, l // tq),
        in_specs=[pl.BlockSpec((None, tq, hp * dq), lambda bi, h, i: (bi, i, h)),
                  pl.BlockSpec((None, l, hp * dq), lambda bi, h, i: (bi, 0, h)),
                  pl.BlockSpec((None, hp * dv, l), lambda bi, h, i: (bi, h, 0))],
        out_specs=pl.BlockSpec((None, tq, hp * dv), lambda bi, h, i: (bi, i, h)),
        out_shape=jax.ShapeDtypeStruct((b, l, n_heads * dv), BF16),
        scratch_shapes=[pltpu.VMEM((hp, 1, tq), F32), pltpu.VMEM((hp, 1, tq), F32),
                        pltpu.VMEM((hp, dv, tq), F32)],
        compiler_params=_cp(("parallel", "parallel", "arbitrary")),
        name="mla_flash",
    )(q, k, vt)


def _decode_kernel(pt_ref, qa_ref, q_ref, cn_ref, kn_ref, ckv_hbm, kpe_hbm, o_ref, ckv_buf, kpe_buf, sem,
                   m_scr, l_scr, acc_scr, *, n_pg, rope, layer):
    s_idx = pl.program_id(1)
    n_steps = pl.num_programs(1)
    g = pl.program_id(0) * n_steps + s_idx
    slot = lax.rem(g, 2)

    def group_copies(gi, sl, lookup):
        out = []
        for p in range(n_pg):
            pg = pt_ref[gi * n_pg + p] if lookup else 0
            out.append(pltpu.make_async_copy(ckv_hbm.at[layer, pg], ckv_buf.at[sl, p], sem.at[sl]))
            out.append(pltpu.make_async_copy(kpe_hbm.at[layer, pg], kpe_buf.at[sl, p], sem.at[sl]))
        return out

    @pl.when(g == 0)
    def _():
        for cp in group_copies(0, 0, True):
            cp.start()

    @pl.when(g + 1 < pl.num_programs(0) * n_steps)
    def _():
        for cp in group_copies(g + 1, 1 - slot, True):
            cp.start()

    for cp in group_copies(g, slot, False):
        cp.wait()

    @pl.when(s_idx == 0)
    def _():
        m_scr[...] = jnp.full_like(m_scr, NEG_INF)
        l_scr[...] = jnp.zeros_like(l_scr)
        acc_scr[...] = jnp.zeros_like(acc_scr)

    qa = qa_ref[...]
    qp = q_ref[...][:, LANES:LANES + rope]
    cks = [ckv_buf[slot, p].astype(BF16) for p in range(n_pg)]
    ss = [_bdot_nt(qa, ck) + _bdot(qp, kpe_buf[slot, p]) for p, ck in enumerate(cks)]
    m_old = m_scr[...]
    m_new = m_old
    for s in ss:
        m_new = jnp.maximum(m_new, jnp.max(s, axis=-1, keepdims=True))
    alpha = jnp.exp(m_old - m_new)
    l_new = alpha * l_scr[...]
    acc = alpha * acc_scr[...]
    for s, ck in zip(ss, cks):
        p = jnp.exp(s - m_new)
        l_new = l_new + jnp.sum(p, axis=-1, keepdims=True)
        acc = acc + _bdot(p, ck)
    m_scr[...] = m_new
    l_scr[...] = l_new
    acc_scr[...] = acc

    @pl.when(s_idx == pl.num_programs(1) - 1)
    def _():
        cn = cn_ref[...].astype(BF16).astype(F32)
        kn = kn_ref[...][:, :rope].astype(BF16).astype(F32)
        s_new = (jnp.sum(qa.astype(F32) * cn, axis=-1, keepdims=True)
                 + jnp.sum(qp.astype(F32) * kn, axis=-1, keepdims=True))
        m_f = jnp.maximum(m_new, s_new)
        a_f = jnp.exp(m_new - m_f)
        p_new = jnp.exp(s_new - m_f)
        l_f = a_f * l_new + p_new
        o = (a_f * acc) / l_f + (p_new / l_f).astype(BF16).astype(F32) * cn
        o_ref[...] = o.astype(o_ref.dtype)


def _decode(page_table, q_abs, q_s, ckv_new, kpe_new, cache_ckv, cache_kpe_t, layer, n_heads, rope):
    db, n_pages = page_table.shape
    kv_lora = q_abs.shape[2]
    page = cache_ckv.shape[2]
    n_pg = math.gcd(n_pages, DECODE_PAGES)
    n_steps = n_pages // n_pg
    pt_flat = page_table.reshape(-1)
    in_specs = [pl.BlockSpec((None, n_heads, kv_lora), lambda b, s, pt: (b, 0, 0)),
                pl.BlockSpec((None, n_heads, 2 * LANES), lambda b, s, pt: (b, 0, 0)),
                pl.BlockSpec((None, 1, kv_lora), lambda b, s, pt: (b, 0, 0)),
                pl.BlockSpec((None, 1, LANES), lambda b, s, pt: (b, 0, 0)),
                pl.BlockSpec(memory_space=pl.ANY), pl.BlockSpec(memory_space=pl.ANY)]
    grid_spec = pltpu.PrefetchScalarGridSpec(
        num_scalar_prefetch=1,
        grid=(db, n_steps),
        in_specs=in_specs,
        out_specs=pl.BlockSpec((None, n_heads, kv_lora), lambda b, s, pt: (b, 0, 0)),
        scratch_shapes=[pltpu.VMEM((2, n_pg, page, kv_lora), cache_ckv.dtype),
                        pltpu.VMEM((2, n_pg, rope, page), cache_kpe_t.dtype),
                        pltpu.SemaphoreType.DMA((2,)),
                        pltpu.VMEM((n_heads, 1), F32), pltpu.VMEM((n_heads, 1), F32),
                        pltpu.VMEM((n_heads, kv_lora), F32)],
    )
    return pl.pallas_call(
        functools.partial(_decode_kernel, n_pg=n_pg, rope=rope, layer=layer),
        grid_spec=grid_spec,
        out_shape=jax.ShapeDtypeStruct((db, n_heads, kv_lora), BF16),
        compiler_params=_cp(("arbitrary", "arbitrary")),
        name="mla_decode",
    )(pt_flat, q_abs, q_s, ckv_new, kpe_new, cache_ckv, cache_kpe_t)


def _gather_rows_kernel(idx_ref, nv_ref, *refs, tm, n_src, split):
    srcs = refs[:n_src]
    o_ref, buf, sem = refs[n_src:]
    i = pl.program_id(0)
    nv = nv_ref[0]
    slot = lax.rem(i, 2)

    def row_copy(src, row, s, r):
        return pltpu.make_async_copy(src.at[pl.ds(row, 1)], buf.at[s, pl.ds(r, 1)], sem.at[s])

    def start_tile(t, s):
        def start_row(r, priority):
            row = idx_ref[t * tm + r]
            if n_src == 1:
                row_copy(srcs[0], row, s, r).start(priority=priority)
            else:
                @pl.when(row < split)
                def _():
                    row_copy(srcs[0], row, s, r).start(priority=priority)

                @pl.when(row >= split)
                def _():
                    row_copy(srcs[1], row - split, s, r).start(priority=priority)

        def start(k, c):
            start_row(2 * k, 0)
            start_row(2 * k + 1, 1)
            return c

        lax.fori_loop(0, tm // 2, start, 0, unroll=GATHER_UNROLL // 2)

    @pl.when((i == 0) & (nv > 0))
    def _():
        start_tile(0, 0)

    @pl.when(i + 1 < nv)
    def _():
        start_tile(i + 1, 1 - slot)

    @pl.when(i < nv)
    def _():
        def wait(r, c):
            row_copy(srcs[0], 0, slot, r).wait()
            return c

        lax.fori_loop(0, tm, wait, 0, unroll=GATHER_UNROLL)
        o_ref[...] = buf[slot].astype(o_ref.dtype)

    @pl.when(i >= nv)
    def _():
        o_ref[...] = jnp.zeros_like(o_ref)


def _gather_rows(idx, n_valid, srcs, tm, out_dtype, name):
    n = idx.shape[0]
    d = srcs[0].shape[1]
    grid_spec = pltpu.PrefetchScalarGridSpec(
        num_scalar_prefetch=2,
        grid=(n // tm,),
        in_specs=[pl.BlockSpec(memory_space=pl.ANY)] * len(srcs),
        out_specs=pl.BlockSpec((tm, d), lambda i, idx, nv: (i, 0)),
        scratch_shapes=[pltpu.VMEM((2, tm, d), srcs[0].dtype), pltpu.SemaphoreType.DMA((2,))],
    )
    return pl.pallas_call(
        functools.partial(_gather_rows_kernel, tm=tm, n_src=len(srcs), split=srcs[0].shape[0]),
        grid_spec=grid_spec,
        out_shape=jax.ShapeDtypeStruct((n, d), out_dtype),
        compiler_params=_cp(("arbitrary",)),
        name=name,
    )(idx, n_valid, *srcs)


def _run_weights(plan_refs, w_hbms, bufs, sem, layer, tw):
    nv_ref, first_ref, run_ref, rune_ref, nr_ref = plan_refs
    j, i = pl.program_id(0), pl.program_id(1)
    nr = nr_ref[0]
    valid = i < nv_ref[0]
    r = run_ref[i]
    slot = lax.rem(j * nr + r, 2)

    def fetch(e, jj, sl):
        cols = pl.ds(pl.multiple_of(jj * tw, tw), tw)
        return [pltpu.make_async_copy(w.at[layer, e, :, cols], b.at[sl], sem.at[sl]) for w, b in zip(w_hbms, bufs)]

    @pl.when(valid & (first_ref[i] == 1))
    def _():
        @pl.when((j == 0) & (r == 0))
        def _():
            for cp in fetch(rune_ref[0], 0, 0):
                cp.start()

        wraps = r + 1 == nr
        rn = jnp.where(wraps, 0, r + 1)
        jn = jnp.where(wraps, j + 1, j)

        @pl.when(jn < pl.num_programs(0))
        def _():
            for cp in fetch(rune_ref[rn], jn, 1 - slot):
                cp.start()

        for cp in fetch(0, 0, slot):
            cp.wait()

    return valid, slot


def _expert_up_kernel(nv_ref, first_ref, run_ref, rune_ref, nr_ref, x_ref, wr_ref, wg_hbm, wu_hbm, o_ref,
                      gbuf, ubuf, sem, *, layer, tf):
    valid, slot = _run_weights((nv_ref, first_ref, run_ref, rune_ref, nr_ref), (wg_hbm, wu_hbm), (gbuf, ubuf),
                               sem, layer, tf)

    @pl.when(valid)
    def _():
        packed = x_ref[...]
        lo = lax.bitcast_convert_type(packed << 16, F32)
        hi = lax.bitcast_convert_type(packed & jnp.uint32(0xFFFF0000), F32)
        x = jnp.concatenate([lo, hi], axis=1).astype(BF16)
        act = jax.nn.silu(_bdot(x, gbuf[slot])) * _bdot(x, ubuf[slot]) * wr_ref[...]
        o_ref[...] = act.astype(o_ref.dtype)

    @pl.when(jnp.logical_not(valid))
    def _():
        o_ref[...] = jnp.zeros_like(o_ref)


def _expert_down_kernel(nv_ref, first_ref, run_ref, rune_ref, nr_ref, h_ref, wd_hbm, o_ref, dbuf, sem,
                        *, layer, tn):
    valid, slot = _run_weights((nv_ref, first_ref, run_ref, rune_ref, nr_ref), (wd_hbm,), (dbuf,), sem, layer, tn)

    @pl.when(valid)
    def _():
        o_ref[...] = _bdot(h_ref[...], dbuf[slot])

    @pl.when(jnp.logical_not(valid))
    def _():
        o_ref[...] = jnp.zeros_like(o_ref)


def _experts(xs, wrow, plan, w_gate, w_up, w_down, layer, tm, tf, tn):
    n, dh = xs.shape
    d = 2 * dh
    n_tiles = n // tm
    ff = w_gate.shape[3]
    hbm = pl.BlockSpec(memory_space=pl.ANY)

    def row_tile(j, i, nv, *_):
        return (jnp.minimum(i, nv[0] - 1), 0)

    up_spec = pltpu.PrefetchScalarGridSpec(
        num_scalar_prefetch=5,
        grid=(ff // tf, n_tiles),
        in_specs=[pl.BlockSpec((tm, dh), row_tile), pl.BlockSpec((tm, 1), row_tile), hbm, hbm],
        out_specs=pl.BlockSpec((tm, tf), lambda j, i, *_: (i, j)),
        scratch_shapes=[pltpu.VMEM((2, d, tf), w_gate.dtype), pltpu.VMEM((2, d, tf), w_up.dtype),
                        pltpu.SemaphoreType.DMA((2,))],
    )
    act = pl.pallas_call(
        functools.partial(_expert_up_kernel, layer=layer, tf=tf),
        grid_spec=up_spec,
        out_shape=jax.ShapeDtypeStruct((n, ff), BF16),
        compiler_params=_cp(("arbitrary", "arbitrary")),
        name="expert_up",
    )(*plan, xs, wrow, w_gate, w_up)
    down_spec = pltpu.PrefetchScalarGridSpec(
        num_scalar_prefetch=5,
        grid=(d // tn, n_tiles),
        in_specs=[pl.BlockSpec((tm, ff), row_tile), hbm],
        out_specs=pl.BlockSpec((tm, tn), lambda j, i, *_: (i, j)),
        scratch_shapes=[pltpu.VMEM((2, ff, tn), w_down.dtype), pltpu.SemaphoreType.DMA((2,))],
    )
    return pl.pallas_call(
        functools.partial(_expert_down_kernel, layer=layer, tn=tn),
        grid_spec=down_spec,
        out_shape=jax.ShapeDtypeStruct((n, d), F32),
        compiler_params=_cp(("arbitrary", "arbitrary")),
        name="expert_down",
    )(*plan, act, w_down)


def _dispatch_plan(e_idx, w_sel, n_exp, tm):
    t = e_idx.shape[1]
    n_tiles = (2 * t + n_exp * (tm - 1)) // tm
    e_flat = e_idx.reshape(-1)
    onehot = (e_flat[:, None] == jnp.arange(n_exp, dtype=I32)[None, :]).astype(I32)
    csum = jnp.cumsum(onehot, axis=0)
    rank = jnp.sum((csum - onehot) * onehot, axis=1)
    tiles_e = (csum[-1] + tm - 1) // tm
    tile_end = jnp.cumsum(tiles_e)
    pos = (tile_end - tiles_e)[e_flat] * tm + rank
    n_valid = tile_end[-1]
    pair = jnp.stack([jnp.tile(jnp.arange(t, dtype=I32), 2),
                      lax.bitcast_convert_type(w_sel.reshape(-1), I32)], axis=1)
    rows = jnp.zeros((n_tiles * tm, 2), I32).at[pos].set(pair)
    src = rows[:, 0]
    wrow = lax.bitcast_convert_type(rows[:, 1], F32)
    tile_ids = jnp.arange(n_tiles, dtype=I32)
    valid = tile_ids < n_valid
    te = jnp.minimum(jnp.searchsorted(tile_end, tile_ids, side="right"), n_exp - 1).astype(I32)
    used = tiles_e > 0
    run_of_expert = jnp.cumsum(used.astype(I32)) - 1
    run = jnp.where(valid, run_of_expert[te], 0).astype(I32)
    first = (valid & (tile_ids == (tile_end - tiles_e)[te])).astype(I32)
    run_expert = jnp.nonzero(used, size=n_exp, fill_value=0)[0].astype(I32)
    n_runs = jnp.sum(used.astype(I32)).reshape(1)
    plan = (n_valid.reshape(1).astype(I32), first, run, run_expert, n_runs)
    return src, wrow.reshape(-1, 1), plan, pos.astype(I32)


def _combine_kernel(x_ref, g_ref, c0_ref, c1_ref, o_ref):
    o_ref[...] = x_ref[...] + g_ref[...] * (c0_ref[...] + c1_ref[...])


def _combine(x, gate, contrib, row0, tl):
    bx, lx, d = x.shape
    nl = lx // tl
    off = row0 // tl
    tok = pl.BlockSpec((None, tl, d), lambda b, i: (b, i, 0))
    return pl.pallas_call(
        _combine_kernel,
        grid=(bx, nl),
        in_specs=[tok, _mod_spec(gate, tl),
                  pl.BlockSpec((None, tl, d), lambda b, i: (0, off + b * nl + i, 0)),
                  pl.BlockSpec((None, tl, d), lambda b, i: (1, off + b * nl + i, 0))],
        out_specs=tok,
        out_shape=jax.ShapeDtypeStruct((bx, lx, d), F32),
        compiler_params=_cp(("parallel", "parallel")),
        name="moe_combine",
    )(x, gate, contrib, contrib)


def _moe(f_p, f_s, route_p, route_s, w_gate, w_up, w_down, layer):
    bp, lp, dh = f_p.shape
    d = 2 * dh
    n_p = bp * lp
    n_s = f_s.shape[1]
    n_exp = w_gate.shape[1]
    e_idx = jnp.concatenate([route_p[0], route_s[0]], axis=1)
    w_sel = jnp.concatenate([route_p[1], route_s[1]], axis=1)
    src, wrow, plan, pos = _dispatch_plan(e_idx, w_sel, n_exp, MOE_TM)
    xs = _gather_rows(src, plan[0], [f_p.reshape(n_p, dh), f_s.reshape(n_s, dh)], MOE_TM, jnp.uint32,
                      "moe_dispatch")
    ys = _experts(xs, wrow, plan, w_gate, w_up, w_down, layer, MOE_TM,
                  min(MOE_TF, w_gate.shape[3]), min(MOE_TN, d))
    t = n_p + n_s
    tg = math.gcd(2 * t, LANES)
    contrib = _gather_rows(pos, jnp.full((1,), 2 * t // tg, I32), [ys], tg, F32, "moe_collect")
    return contrib.reshape(2, t, d)


def kernel(x_prompt, x_sample, c_prompt, c_sample, state_ret, state_s5_re, state_s5_im, cache_ckv, cache_kpe, page_table, w_ada, b_ada, norm_mix, norm_ffn, norm_final, w_in_ab, ret_gn, s5_lam_re, s5_lam_im, s5_log_dt, s5_b_re, s5_b_im, s5_c_re, s5_c_im, s5_d, s5_w_glu, s5_b_glu, w_out_ab, w_in_c, q_norm, kv_norm, w_q_up, w_kv_up, w_out_c, w_router, b_router, w_exp_gate, w_exp_up, w_exp_down):
    bp, lp, d = x_prompt.shape
    db = x_sample.shape[0]
    assert x_sample.shape[1] == 1
    depth = w_ada.shape[0]
    past_len = page_table.shape[1] * cache_ckv.shape[2]
    _, _, ret_heads, ret_dk, ret_dv = state_ret.shape
    assert ret_dk == ret_dv and ret_dk % (2 * LANES) == 0
    ret_width = ret_heads * ret_dk
    s5_width = s5_w_glu.shape[1]
    mla_heads = w_q_up.shape[2]
    q_lora = w_q_up.shape[1]
    kv_lora = w_kv_up.shape[1]
    qk_rope = cache_kpe.shape[3]
    qk_nope = w_q_up.shape[3] - qk_rope
    v_head = w_kv_up.shape[3] - qk_nope
    assert qk_nope == LANES and v_head == LANES and qk_rope == LANES // 2
    attn_scale = (qk_nope + qk_rope) ** -0.5

    tl_p = min(256, lp)
    tm_p = min(1024, lp)
    tq = min(512, lp)

    xp = x_prompt
    xs = x_sample.reshape(1, db, d)

    mod = _adaln(jnp.concatenate([c_prompt, c_sample], axis=0), w_ada, b_ada)

    def mods(i):
        mp = [mod[i, :bp, k * d:(k + 1) * d].reshape(bp, 1, d) for k in range(6)]
        ms = [mod[i, bp:, k * d:(k + 1) * d].reshape(1, db, d) for k in range(6)]
        return mp, ms

    pos_p = jnp.arange(lp, dtype=F32)
    pos_s = past_len + jnp.arange(1, dtype=F32)
    w_router_t = w_router.T
    outs = {k: [] for k in ("ret_p", "ret_s", "s5re_p", "s5im_p", "s5re_s", "s5im_s",
                            "ckv_p", "kpe_p", "ckv_s", "kpe_s")}

    for i in range(depth):
        (sh1_p, sc1_p, g1_p, sh2_p, sc2_p, g2_p), (sh1_s, sc1_s, g1_s, sh2_s, sc2_s, g2_s) = mods(i)
        h_p = _norm_mod(xp, norm_mix[i], sh1_p, sc1_p, tl_p)
        h_s = _norm_mod(xs, norm_mix[i], sh1_s, sc1_s, db)
        j = i // 2
        if i % 2 == 0:
            w_in = (w_in_ab, j)
            tn_in = 512
            proj_p = _mm([h_p], w_in, tm=tm_p, tn=tn_in, out_dtype=F32, name="ab_in_p")
            proj_s = _mm([h_s], w_in, tm=db, tn=tn_in, out_dtype=F32, name="ab_in_s")[0]
            chunk = min(RET_CHUNK, lp)
            cos_p, sin_p = _rope_tables(pos_p, ret_dk // 2)
            ret_out_p, ret_p = _retention(proj_p, cos_p, sin_p, _ret_tables(ret_heads, chunk, chunk),
                                          ret_gn[j], None, ret_heads, ret_dk, chunk, math.gcd(ret_heads, 2))
            cos_s, sin_s = _rope_tables(pos_s, ret_dk // 2)
            pad_rows = ((0, RET_PAD - 1), (0, 0))
            proj_s8 = jnp.pad(proj_s[:, None, :4 * ret_width], ((0, 0), (0, RET_PAD - 1), (0, 0)))
            ret_out_s8, ret_s = _retention(proj_s8, jnp.pad(cos_s, pad_rows), jnp.pad(sin_s, pad_rows),
                                           _ret_tables(ret_heads, 1, RET_PAD), ret_gn[j], state_ret[j],
                                           ret_heads, ret_dk, RET_PAD, ret_heads)
            ret_out_s = ret_out_s8[:, 0, :].reshape(1, db, ret_width)
            params = _s5_params(s5_lam_re[j], s5_lam_im[j], s5_log_dt[j], s5_b_re[j], s5_b_im[j],
                                s5_c_re[j], s5_c_im[j])
            col0 = 4 * ret_width // LANES
            z_p, s5re_p, s5im_p = _s5_scan(proj_p, col0, params, s5_d[j], None)
            n_grp, n_state = state_s5_re.shape[2:]
            z_s, s5re_s, s5im_s = _s5_step(proj_s, col0, params, s5_d[j],
                                           state_s5_re[j].reshape(db, -1), state_s5_im[j].reshape(db, -1))
            z_s = z_s.reshape(1, db, s5_width)

            def glu(z, tm):
                tn = min(512, s5_width)
                specs = (pl.BlockSpec((None, tm, tn), lambda b, i_, j_: (b, i_, j_)),
                         pl.BlockSpec((1, tn), lambda b, i_, j_: (0, j_)))
                return _mm([z], (s5_w_glu, j), tm=tm, tn=tn, out_dtype=BF16, epi="glu",
                           extra=(z, s5_b_glu[j].reshape(1, -1)), extra_specs=specs, name="s5_glu")

            s5_out_p = glu(z_p, tm_p)
            s5_out_s = glu(z_s, db)
            xp = _mm_resgate([ret_out_p, s5_out_p], (w_out_ab, j), xp, g1_p, tm_p, 512, "ab_out_p")
            xs = _mm_resgate([ret_out_s, s5_out_s], (w_out_ab, j), xs, g1_s, db, 512, "ab_out_s")
            outs["ret_p"].append(ret_p)
            outs["ret_s"].append(ret_s)
            outs["s5re_p"].append(s5re_p.reshape(bp, n_grp, n_state))
            outs["s5im_p"].append(s5im_p.reshape(bp, n_grp, n_state))
            outs["s5re_s"].append(s5re_s.reshape(db, n_grp, n_state))
            outs["s5im_s"].append(s5im_s.reshape(db, n_grp, n_state))
        else:
            wc = w_in_c[j]
            w_kpe = wc[:, q_lora + kv_lora:]
            half = qk_rope // 2
            w_c = jnp.concatenate([wc, -w_kpe[:, half:], w_kpe[:, :half]], axis=1)
            wq = w_q_up[j]
            wq_pe = wq[:, :, qk_nope:]
            w_q = jnp.concatenate([wq, -wq_pe[:, :, half:], wq_pe[:, :, :half]], axis=2)
            w_q = w_q.reshape(q_lora, mla_heads * 2 * LANES)
            wkv = w_kv_up[j]
            w_k = wkv[:, :, :qk_nope].reshape(kv_lora, mla_heads * qk_nope)
            w_vt = wkv[:, :, qk_nope:].transpose(1, 2, 0)
            w_uk_t = wkv[:, :, :qk_nope].transpose(1, 2, 0)
            w_uv = wkv[:, :, qk_nope:].transpose(1, 0, 2)

            def pe_tables(pos):
                cos, sin = _rope_tables(pos, half)
                z = jnp.zeros((pos.shape[0], LANES - qk_rope), F32)
                return (jnp.concatenate([cos, cos, z], axis=1), jnp.concatenate([sin, sin, z], axis=1))

            def project(h, pos, tm):
                cos_t, sin_t = pe_tables(pos)
                q_lat, ckv, kpe = _mla_in(h, w_c, q_norm[j], kv_norm[j], cos_t, sin_t, q_lora, kv_lora,
                                          tm, min(512, d))
                if pos.shape[0] == 1:
                    tab = pl.BlockSpec((1, LANES), lambda b, i_, j_: (0, 0))
                else:
                    tab = pl.BlockSpec((tm, LANES), lambda b, i_, j_: (i_, 0))
                q = _mm([q_lat], w_q, tm=tm, tn=2 * LANES, out_dtype=BF16, epi="rope", out_scale=attn_scale,
                        extra=(cos_t, sin_t), extra_specs=(tab, tab), name="mla_q_up")
                return q, ckv, kpe

            q_p, ckv_p, kpe_p = project(h_p, pos_p, tm_p)
            kspec = (pl.BlockSpec((None, tm_p, LANES), lambda b, i_, j_: (b, i_, 0)),)
            k_p = _mm([ckv_p], w_k, tm=tm_p, tn=LANES, tn_out=2 * LANES, out_dtype=BF16, epi="kcat",
                      extra=(kpe_p,), extra_specs=kspec, name="mla_k_up")
            vt_p = _vt_up(ckv_p, w_vt, tm_p)
            att_p = _flash(q_p, k_p, vt_p, mla_heads, 2 * LANES, v_head, tq)
            xp = _mm_resgate([att_p], (w_out_c, j), xp, g1_p, tm_p, 512, "c_out_p")
            q_s, ckv_s, kpe_s = project(h_s, pos_s, db)
            q_s2 = q_s.reshape(db, mla_heads * 2 * LANES)
            q_abs = _bmm_heads(q_s2, w_uk_t, lambda h_: 2 * h_, BF16, "mla_q_absorb")
            o_lat = _decode(page_table, q_abs.reshape(db, mla_heads, kv_lora),
                            q_s2.reshape(db, mla_heads, 2 * LANES), ckv_s.reshape(db, 1, kv_lora),
                            kpe_s.reshape(db, 1, LANES), cache_ckv, jnp.swapaxes(cache_kpe, 2, 3), j, mla_heads,
                            qk_rope)
            att_s = _bmm_heads(o_lat.reshape(db, mla_heads * kv_lora), w_uv, lambda h_: h_, BF16, "mla_v_absorb")
            xs = _mm_resgate([att_s.reshape(1, db, mla_heads * v_head)], (w_out_c, j), xs, g1_s, db, 512,
                             "c_out_s")
            outs["ckv_p"].append(ckv_p)
            outs["kpe_p"].append(kpe_p[:, :, :qk_rope])
            outs["ckv_s"].append(ckv_s.reshape(db, 1, kv_lora))
            outs["kpe_s"].append(kpe_s.reshape(db, 1, LANES)[:, :, :qk_rope])

        f_p, e_p, wsel_p = _ffn_norm_route(xp, norm_ffn[i], sh2_p, sc2_p, w_router_t, b_router, tl_p)
        f_s, e_s, wsel_s = _ffn_norm_route(xs, norm_ffn[i], sh2_s, sc2_s, w_router_t, b_router, db)
        contrib = _moe(f_p, f_s, (e_p, wsel_p), (e_s, wsel_s), w_exp_gate, w_exp_up, w_exp_down, i)
        xp = _combine(xp, g2_p, contrib, 0, tl_p)
        xs = _combine(xs, g2_s, contrib, bp * lp, db)

    y_prompt = _rmsnorm(xp, norm_final, tl_p)
    y_sample = _rmsnorm(xs, norm_final, db).reshape(db, 1, d)
    st = lambda k: jnp.stack(outs[k])
    return (y_prompt, y_sample, st("ret_p"), st("ret_s"), st("s5re_p"), st("s5im_p"), st("s5re_s"),
            st("s5im_s"), st("ckv_p"), st("kpe_p"), st("ckv_s"), st("kpe_s"))
```
